```python
import math
import jax, jax.numpy as jnp
from jax import lax
import numpy as np

D_MODEL = 1024
BATCH = 2
SEQ = 16384
DEPTH = 2

HEAD_DIM = 64
D_MIX = D_MODEL
N_FOX = (3 * D_MIX // 8) // HEAD_DIM
N_MOBA = (3 * D_MIX // 8) // HEAD_DIM
D_FOX = N_FOX * HEAD_DIM
D_MOBA = N_MOBA * HEAD_DIM
D_LRU = D_MIX - D_FOX - D_MOBA
LRU_GROUP = HEAD_DIM
N_LRU_GROUPS = D_LRU // LRU_GROUP
N_OUT_HEADS = D_MIX // HEAD_DIM
Q_BLOCK = 128
MOBA_BLOCK = 256
MOBA_TOPK = 3
MOBA_Q_CHUNK = 64
CONV_WIDTH = 4
LRU_C = 8.0
D_FF = -(-8 * D_MODEL // (3 * 256)) * 256
ALPHA = (2 * DEPTH) ** 0.25
BETA = (8 * DEPTH) ** -0.25
LN_EPS = 1e-5
RMS_EPS = 1e-6
NEG = -1e30
IN_SIZES = [D_FOX, D_FOX, D_FOX, N_FOX,
            D_MOBA, D_MOBA, D_MOBA,
            D_LRU, D_LRU]
N_IN = sum(IN_SIZES)
IN_SPLITS = [int(s) for s in np.cumsum(IN_SIZES)[:-1]]

kernel_name = "hymba_fox_moba_rglru_deepnorm"


def layer_norm(x, g, b):
    xf = x.astype(jnp.float32)
    mu = jnp.mean(xf, axis=-1, keepdims=True)
    var = jnp.mean(jnp.square(xf - mu), axis=-1, keepdims=True)
    y = (xf - mu) * lax.rsqrt(var + LN_EPS)
    return (y * g.astype(jnp.float32) + b.astype(jnp.float32)).astype(x.dtype)


def fox_attention(q, k, v, log_f):
    B, S, H, dh = q.shape
    scale = 1.0 / math.sqrt(dh)
    F = jnp.cumsum(log_f, axis=1).transpose(0, 2, 1)
    nblk = S // Q_BLOCK
    qb = q.reshape(B, nblk, Q_BLOCK, H, dh).transpose(1, 0, 2, 3, 4)
    Fb = F.reshape(B, H, nblk, Q_BLOCK).transpose(2, 0, 1, 3)
    kpos = jnp.arange(S)

    def one_block(args):
        i, q_i, F_i = args
        s = jnp.einsum('bqhd,bkhd->bhqk', q_i, k).astype(jnp.float32) * scale
        s = s + F_i[..., :, None] - F[:, :, None, :]
        qpos = i * Q_BLOCK + jnp.arange(Q_BLOCK)
        causal = kpos[None, :] <= qpos[:, None]
        s = jnp.where(causal, s, NEG)
        p = jax.nn.softmax(s, axis=-1).astype(v.dtype)
        return jnp.einsum('bhqk,bkhd->bqhd', p, v)

    out = lax.map(one_block, (jnp.arange(nblk), qb, Fb))
    return out.transpose(1, 0, 2, 3, 4).reshape(B, S, H * dh)


def moba_attention(q, k, v):
    B, S, H, dh = q.shape
    scale = 1.0 / math.sqrt(dh)
    nb = max(-(-S // MOBA_BLOCK), MOBA_TOPK)
    pad = nb * MOBA_BLOCK - S
    kp = jnp.pad(k, ((0, 0), (0, pad), (0, 0), (0, 0)))
    vp = jnp.pad(v, ((0, 0), (0, pad), (0, 0), (0, 0)))
    kblk = kp.reshape(B, nb, MOBA_BLOCK, H, dh)
    vblk = vp.reshape(B, nb, MOBA_BLOCK, H, dh)
    kmean = jnp.mean(kblk.astype(jnp.float32), axis=2).astype(k.dtype)
    nq = S // MOBA_Q_CHUNK
    qc = q.reshape(B, nq, MOBA_Q_CHUNK, H, dh).transpose(1, 0, 2, 3, 4)
    bi = jnp.arange(B)[:, None, None, None]
    hi = jnp.arange(H)[None, None, :, None]
    blk_ids = jnp.arange(nb)
    own_off = jnp.arange(MOBA_BLOCK)

    def one_chunk(args):
        c, q_c = args
        start = c * MOBA_Q_CHUNK
        cur = start // MOBA_BLOCK
        qpos = start + jnp.arange(MOBA_Q_CHUNK)
        g = jnp.einsum('bqhd,bnhd->bqhn', q_c, kmean).astype(jnp.float32)
        g = jnp.where(blk_ids < cur, g, NEG)
        _, idx = lax.top_k(g, MOBA_TOPK)
        valid = idx < cur
        gk = kblk[bi, idx, :, hi]
        gv = vblk[bi, idx, :, hi]
        s_sel = jnp.einsum('bqhd,bqhnkd->bqhnk', q_c, gk).astype(jnp.float32) * scale
        s_sel = jnp.where(valid[..., None], s_sel, NEG)
        s_sel = s_sel.reshape(B, MOBA_Q_CHUNK, H, MOBA_TOPK * MOBA_BLOCK)
        k_own = lax.dynamic_slice_in_dim(kp, cur * MOBA_BLOCK, MOBA_BLOCK, axis=1)
        v_own = lax.dynamic_slice_in_dim(vp, cur * MOBA_BLOCK, MOBA_BLOCK, axis=1)
        s_own = jnp.einsum('bqhd,bkhd->bqhk', q_c, k_own).astype(jnp.float32) * scale
        kpos_own = cur * MOBA_BLOCK + own_off
        causal = kpos_own[None, :] <= qpos[:, None]
        s_own = jnp.where(causal[None, :, None, :], s_own, NEG)
        p = jax.nn.softmax(jnp.concatenate([s_sel, s_own], axis=-1), axis=-1).astype(v.dtype)
        p_sel = p[..., :MOBA_TOPK * MOBA_BLOCK].reshape(B, MOBA_Q_CHUNK, H, MOBA_TOPK, MOBA_BLOCK)
        p_own = p[..., MOBA_TOPK * MOBA_BLOCK:]
        return (jnp.einsum('bqhnk,bqhnkd->bqhd', p_sel, gv)
                + jnp.einsum('bqhk,bkhd->bqhd', p_own, v_own))

    out = lax.map(one_chunk, (jnp.arange(nq), qc))
    return out.transpose(1, 0, 2, 3, 4).reshape(B, S, H * dh)


def rglru_branch(xb, conv_w, conv_b, w_r, b_r, w_i, b_i, lam):
    B, S, _ = xb.shape
    xp = jnp.pad(xb, ((0, 0), (CONV_WIDTH - 1, 0), (0, 0)))
    xc = conv_b + sum(xp[:, j:j + S] * conv_w[j] for j in range(CONV_WIDTH))
    xg = xc.reshape(B, S, N_LRU_GROUPS, LRU_GROUP)
    r = jax.nn.sigmoid(jnp.einsum('bsgi,gij->bsgj', xg, w_r).reshape(B, S, D_LRU) + b_r)
    i = jax.nn.sigmoid(jnp.einsum('bsgi,gij->bsgj', xg, w_i).reshape(B, S, D_LRU) + b_i)
    log_a = LRU_C * r.astype(jnp.float32) * jax.nn.log_sigmoid(lam.astype(jnp.float32))
    a = jnp.exp(log_a)
    u = jnp.sqrt(-jnp.expm1(2.0 * log_a)) * (i * xc).astype(jnp.float32)

    def combine(e1, e2):
        a1, b1 = e1
        a2, b2 = e2
        return a1 * a2, a2 * b1 + b2

    _, h = lax.associative_scan(combine, (a, u), axis=1)
    return h.astype(xb.dtype)


def headwise_rmsnorm(y, g):
    B, S, _ = y.shape
    yf = y.astype(jnp.float32).reshape(B, S, N_OUT_HEADS, HEAD_DIM)
    yf = yf * lax.rsqrt(jnp.mean(jnp.square(yf), axis=-1, keepdims=True) + RMS_EPS)
    return (yf.reshape(B, S, D_MIX) * g.astype(jnp.float32)).astype(y.dtype)


def setup_inputs(seed: int = 0) -> dict:
    key = jax.random.key(seed)
    ks = jax.random.split(key, 24)
    L = DEPTH
    nrm = jax.random.normal
    x = nrm(ks[0], (BATCH, SEQ, D_MODEL), jnp.float32)
    col_scale = jnp.concatenate([
        jnp.ones((3 * D_FOX,), jnp.float32),
        jnp.full((N_FOX,), 0.1, jnp.float32),
        jnp.ones((3 * D_MOBA + 2 * D_LRU,), jnp.float32)])
    w_in = nrm(ks[1], (L, D_MODEL, N_IN), jnp.float32) * (D_MODEL ** -0.5) * col_scale
    b_fgate = jax.random.uniform(ks[2], (L, N_FOX), jnp.float32, 1.0, 6.0)
    conv_w = nrm(ks[3], (L, CONV_WIDTH, D_LRU), jnp.float32) * (CONV_WIDTH ** -0.5)
    conv_b = 0.01 * nrm(ks[4], (L, D_LRU), jnp.float32)
    w_rgate = nrm(ks[5], (L, N_LRU_GROUPS, LRU_GROUP, LRU_GROUP), jnp.float32) * (LRU_GROUP ** -0.5)
    b_rgate = 0.01 * nrm(ks[6], (L, D_LRU), jnp.float32)
    w_igate = nrm(ks[7], (L, N_LRU_GROUPS, LRU_GROUP, LRU_GROUP), jnp.float32) * (LRU_GROUP ** -0.5)
    b_igate = 0.01 * nrm(ks[8], (L, D_LRU), jnp.float32)
    a8 = jax.random.uniform(ks[9], (L, D_LRU), jnp.float32, 0.9, 0.999)
    p = a8 ** (1.0 / LRU_C)
    lru_lambda = jnp.log(p) - jnp.log1p(-p)
    out_norm_g = 1.0 + 0.02 * nrm(ks[10], (L, D_MIX), jnp.float32)
    w_out = nrm(ks[11], (L, D_MIX, D_MODEL), jnp.float32) * (D_MIX ** -0.5) * BETA
    ln1_g = 1.0 + 0.02 * nrm(ks[12], (L, D_MODEL), jnp.float32)
    ln1_b = 0.01 * nrm(ks[13], (L, D_MODEL), jnp.float32)
    w_ffn_gate = nrm(ks[14], (L, D_MODEL, D_FF), jnp.float32) * (D_MODEL ** -0.5)
    w_ffn_up = nrm(ks[15], (L, D_MODEL, D_FF), jnp.float32) * (D_MODEL ** -0.5)
    w_ffn_down = nrm(ks[16], (L, D_FF, D_MODEL), jnp.float32) * (D_FF ** -0.5) * BETA
    ln2_g = 1.0 + 0.02 * nrm(ks[17], (L, D_MODEL), jnp.float32)
    ln2_b = 0.01 * nrm(ks[18], (L, D_MODEL), jnp.float32)
    return {'x': x, 'w_in': w_in, 'b_fgate': b_fgate, 'conv_w': conv_w, 'conv_b': conv_b,
            'w_rgate': w_rgate, 'b_rgate': b_rgate, 'w_igate': w_igate, 'b_igate': b_igate,
            'lru_lambda': lru_lambda, 'out_norm_g': out_norm_g, 'w_out': w_out,
            'ln1_g': ln1_g, 'ln1_b': ln1_b, 'w_ffn_gate': w_ffn_gate, 'w_ffn_up': w_ffn_up,
            'w_ffn_down': w_ffn_down, 'ln2_g': ln2_g, 'ln2_b': ln2_b}


def reference(x, w_in, b_fgate, conv_w, conv_b, w_rgate, b_rgate, w_igate, b_igate,
              lru_lambda, out_norm_g, w_out, ln1_g, ln1_b, w_ffn_gate, w_ffn_up,
              w_ffn_down, ln2_g, ln2_b):
    B, S, _ = x.shape
    for l in range(DEPTH):
        proj = jnp.einsum('bsd,dn->bsn', x, w_in[l])
        fq, fk, fv, ff, mq, mk, mv, lx, lg = jnp.split(proj, IN_SPLITS, axis=-1)
        heads = lambda t, h: t.reshape(B, S, h, HEAD_DIM)
        log_f = jax.nn.log_sigmoid((ff + b_fgate[l]).astype(jnp.float32))
        y_fox = fox_attention(heads(fq, N_FOX), heads(fk, N_FOX), heads(fv, N_FOX), log_f)
        y_moba = moba_attention(heads(mq, N_MOBA), heads(mk, N_MOBA), heads(mv, N_MOBA))
        h_lru = rglru_branch(lx, conv_w[l], conv_b[l], w_rgate[l], b_rgate[l],
                             w_igate[l], b_igate[l], lru_lambda[l])
        y_lru = jax.nn.gelu(lg) * h_lru
        y = jnp.concatenate([y_fox, y_moba, y_lru], axis=-1)
        y = headwise_rmsnorm(y, out_norm_g[l])
        mix = jnp.einsum('bsm,md->bsd', y, w_out[l])
        x = layer_norm(ALPHA * x + mix, ln1_g[l], ln1_b[l])
        hid = jax.nn.silu(jnp.einsum('bsd,df->bsf', x, w_ffn_gate[l])) * jnp.einsum('bsd,df->bsf', x, w_ffn_up[l])
        ffn = jnp.einsum('bsf,fd->bsd', hid, w_ffn_down[l])
        x = layer_norm(ALPHA * x + ffn, ln2_g[l], ln2_b[l])
    return x
```

```python
import functools
import math

import jax
import jax.numpy as jnp
from jax import lax
from jax.experimental import pallas as pl
from jax.experimental.pallas import tpu as pltpu

F32 = jnp.float32
BF16 = jnp.bfloat16

HEAD_DIM = 64
AUG = 128
N_FOX = 6
N_MOBA = 6
N_ATT = N_FOX + N_MOBA
D_ATT = N_ATT * HEAD_DIM
D_LRU = 256
LRU_GROUP = 64
MOBA_BLOCK = 256
MOBA_TOPK = 3
CONV_WIDTH = 4
LRU_C = 8.0
LN_EPS = 1e-5
RMS_EPS = 1e-6
NEG = -1e30
SCALE = 1.0 / math.sqrt(HEAD_DIM)
N_SPLIT = 3

V7X_VMEM_LIMIT_BYTES = 48 * 1024 * 1024

NT_DIMS = (((1,), (1,)), ((), ()))


def _split3(x):
    hi = x.astype(BF16)
    r1 = x - hi.astype(F32)
    mid = r1.astype(BF16)
    lo = (r1 - mid.astype(F32)).astype(BF16)
    return hi, mid, lo


def _layer_norm(z, g, b):
    mu = jnp.mean(z, axis=-1, keepdims=True)
    zc = z - mu
    var = jnp.mean(zc * zc, axis=-1, keepdims=True)
    return zc * lax.rsqrt(var + LN_EPS) * g + b


def _const_spec(shape):
    nd = len(shape)
    return pl.BlockSpec(shape, lambda *_: (0,) * nd, pipeline_mode=pl.Buffered(1))


def _inproj_kernel(x_ref, wt_ref, wn_ref, qT_ref, k_ref, vT_ref, ff_ref, lx_ref, lg_ref, *, ts):
    si = pl.program_id(1)
    xb = x_ref[0].astype(BF16)
    t = lax.dot_general(wt_ref[...], xb, NT_DIMS, preferred_element_type=F32)
    row = lax.broadcasted_iota(jnp.int32, (AUG, ts), 0)
    ones_rows = (row >= HEAD_DIM) & (row < HEAD_DIM + N_SPLIT)
    for j in range(N_ATT):
        q = t[j * AUG:(j + 1) * AUG, :] * SCALE
        if j < N_FOX:
            q = jnp.where(ones_rows, 1.0, q)
        qT_ref[0, j * AUG:(j + 1) * AUG, :] = q.astype(BF16)
    vT_ref[0] = t[N_ATT * AUG:, :].astype(BF16)

    n = jnp.dot(xb, wn_ref[...], preferred_element_type=F32)
    lane = lax.broadcasted_iota(jnp.int32, (ts, AUG), 1)
    blk = (si * ts + lax.broadcasted_iota(jnp.int32, (ts, AUG), 0)) // MOBA_BLOCK
    onehot = lane - HEAD_DIM == blk
    for j in range(N_ATT):
        k = n[:, j * AUG:(j + 1) * AUG]
        if j >= N_FOX:
            k = jnp.where(onehot, 1.0, k)
        k_ref[0, :, j * AUG:(j + 1) * AUG] = k.astype(BF16)
    base = N_ATT * AUG
    ff_ref[0] = n[:, base:base + AUG]
    lx_ref[0] = n[:, base + AUG:base + AUG + D_LRU]
    lg_ref[0] = n[:, base + AUG + D_LRU:base + AUG + 2 * D_LRU]


def _inproj(x, wt, wn, *, ts):
    B, S, D = x.shape
    grid = (B, S // ts)
    out_shape = (
        jax.ShapeDtypeStruct((B, N_ATT * AUG, S), BF16),
        jax.ShapeDtypeStruct((B, S, N_ATT * AUG), BF16),
        jax.ShapeDtypeStruct((B, D_ATT, S), BF16),
        jax.ShapeDtypeStruct((B, S, AUG), F32),
        jax.ShapeDtypeStruct((B, S, D_LRU), F32),
        jax.ShapeDtypeStruct((B, S, D_LRU), F32),
    )
    return pl.pallas_call(
        functools.partial(_inproj_kernel, ts=ts),
        grid=grid,
        in_specs=[
            pl.BlockSpec((1, ts, D), lambda b, s: (b, s, 0)),
            _const_spec(wt.shape),
            _const_spec(wn.shape),
        ],
        out_specs=(
            pl.BlockSpec((1, N_ATT * AUG, ts), lambda b, s: (b, 0, s)),
            pl.BlockSpec((1, ts, N_ATT * AUG), lambda b, s: (b, s, 0)),
            pl.BlockSpec((1, D_ATT, ts), lambda b, s: (b, 0, s)),
            pl.BlockSpec((1, ts, AUG), lambda b, s: (b, s, 0)),
            pl.BlockSpec((1, ts, D_LRU), lambda b, s: (b, s, 0)),
            pl.BlockSpec((1, ts, D_LRU), lambda b, s: (b, s, 0)),
        ),
        out_shape=out_shape,
        compiler_params=pltpu.CompilerParams(
            dimension_semantics=("arbitrary", "arbitrary"), vmem_limit_bytes=V7X_VMEM_LIMIT_BYTES),
        name="inproj",
    )(x, wt, wn)


def _foxprep_kernel(ff_ref, bf_ref, place_ref, k_ref, out_ref, tri_ref, carry_ref, *, ts):
    @pl.when(pl.program_id(1) == 0)
    def _():
        r = lax.broadcasted_iota(jnp.int32, (ts, ts), 0)
        c = lax.broadcasted_iota(jnp.int32, (ts, ts), 1)
        tri_ref[...] = (c <= r).astype(BF16)
        carry_ref[...] = jnp.zeros_like(carry_ref)

    log_f = jax.nn.log_sigmoid(ff_ref[0] + bf_ref[...])
    tri = tri_ref[...]
    cum = carry_ref[0:1, :]
    for piece in _split3(log_f):
        cum = cum + jnp.dot(tri, piece, preferred_element_type=F32)
    carry_ref[...] = jnp.broadcast_to(cum[ts - 1:ts, :], carry_ref.shape)

    pieces = jnp.concatenate(_split3(-cum), axis=1)
    bias = jnp.dot(pieces, place_ref[...], preferred_element_type=F32)
    lane = lax.broadcasted_iota(jnp.int32, (ts, N_FOX * AUG), 1) % AUG
    out_ref[0] = jnp.where(lane < HEAD_DIM, k_ref[0], bias.astype(BF16))


def _foxprep(ff, bf_pad, place, k_aug, *, ts):
    B, S, _ = ff.shape
    wf = N_FOX * AUG
    return pl.pallas_call(
        functools.partial(_foxprep_kernel, ts=ts),
        grid=(B, S // ts),
        in_specs=[
            pl.BlockSpec((1, ts, AUG), lambda b, s: (b, s, 0)),
            _const_spec(bf_pad.shape),
            _const_spec(place.shape),
            pl.BlockSpec((1, ts, wf), lambda b, s: (b, s, 0)),
        ],
        out_specs=pl.BlockSpec((1, ts, wf), lambda b, s: (b, s, 0)),
        out_shape=jax.ShapeDtypeStruct(k_aug.shape, k_aug.dtype),
        scratch_shapes=[pltpu.VMEM((ts, ts), BF16), pltpu.VMEM((8, AUG), F32)],
        input_output_aliases={3: 0},
        compiler_params=pltpu.CompilerParams(
            dimension_semantics=("arbitrary", "arbitrary"), vmem_limit_bytes=V7X_VMEM_LIMIT_BYTES),
        name="foxprep",
    )(ff, bf_pad, place, k_aug)


def _kmean_kernel(k_ref, out_ref, *, nblk):
    k = k_ref[0].astype(F32)
    out_ref[0] = jnp.mean(k.reshape(nblk, MOBA_BLOCK, k.shape[-1]), axis=1)


def _kmean(k_aug, *, nblk):
    B, S, _ = k_aug.shape
    nb = S // MOBA_BLOCK
    wm = N_MOBA * AUG
    return pl.pallas_call(
        functools.partial(_kmean_kernel, nblk=nblk),
        grid=(B, nb // nblk),
        in_specs=[pl.BlockSpec((1, nblk * MOBA_BLOCK, wm), lambda b, i: (b, i, 1))],
        out_specs=pl.BlockSpec((1, nblk, wm), lambda b, i: (b, i, 0)),
        out_shape=jax.ShapeDtypeStruct((B, nb, wm), F32),
        compiler_params=pltpu.CompilerParams(
            dimension_semantics=("arbitrary", "arbitrary"), vmem_limit_bytes=V7X_VMEM_LIMIT_BYTES),
        name="kmean",
    )(k_aug)


def _mobasel_kernel(kmean_ref, qT_ref, out_ref, *, nb, tq):
    cur = (pl.program_id(2) * tq) // MOBA_BLOCK
    lane = lax.broadcasted_iota(jnp.int32, (nb, AUG), 1)
    kmean = jnp.where(lane < HEAD_DIM, kmean_ref[0], 0.0).astype(BF16)
    qT = qT_ref[0]
    g = jnp.dot(kmean, qT, preferred_element_type=F32)
    blk = lax.broadcasted_iota(jnp.int32, (nb, tq), 0)
    past = blk < cur
    g = jnp.where(past, g, NEG)
    chosen = jnp.zeros((nb, tq), jnp.bool_)
    for _ in range(MOBA_TOPK):
        top = jnp.max(g, axis=0, keepdims=True)
        first = jnp.min(jnp.where(g == top, blk, nb), axis=0, keepdims=True)
        pick = blk == first
        chosen = chosen | pick
        g = jnp.where(pick, -jnp.inf, g)
    allowed = (chosen & past) | (blk == cur)
    out_ref[0, 0:HEAD_DIM, :] = qT[0:HEAD_DIM, :]
    bias = jnp.where(allowed, 0.0, NEG).astype(BF16)
    if nb < AUG - HEAD_DIM:
        bias = jnp.concatenate([bias, jnp.zeros((AUG - HEAD_DIM - nb, tq), BF16)], axis=0)
    out_ref[0, HEAD_DIM:AUG, :] = bias


def _mobasel(kmean, qT_aug, *, tq):
    B, nb, _ = kmean.shape
    S = qT_aug.shape[-1]
    return pl.pallas_call(
        functools.partial(_mobasel_kernel, nb=nb, tq=tq),
        grid=(B, N_MOBA, S // tq),
        in_specs=[
            pl.BlockSpec((1, nb, AUG), lambda b, h, i: (b, 0, h)),
            pl.BlockSpec((1, AUG, tq), lambda b, h, i: (b, N_FOX + h, i)),
        ],
        out_specs=pl.BlockSpec((1, AUG, tq), lambda b, h, i: (b, N_FOX + h, i)),
        out_shape=jax.ShapeDtypeStruct(qT_aug.shape, qT_aug.dtype),
        input_output_aliases={1: 0},
        compiler_params=pltpu.CompilerParams(
            dimension_semantics=("arbitrary", "arbitrary", "arbitrary"),
            vmem_limit_bytes=V7X_VMEM_LIMIT_BYTES),
        name="mobasel",
    )(kmean, qT_aug)


def _attn_kernel(qT_ref, k_ref, vT_ref, g_ref, out_ref, *, tq):
    qi = pl.program_id(2)
    tk = tq
    key = lax.broadcasted_iota(jnp.int32, (tk, tq), 0)
    qry = lax.broadcasted_iota(jnp.int32, (tk, tq), 1)
    causal = key <= qry

    def head(hh):
        qT = qT_ref[0, hh * AUG:(hh + 1) * AUG, :]

        def tile(j, carry, masked):
            m, l, acc = carry
            off = pl.multiple_of(j * tk, tk)
            k = k_ref[0, pl.ds(off, tk), hh * AUG:(hh + 1) * AUG]
            s = jnp.dot(k, qT, preferred_element_type=F32)
            if masked:
                s = jnp.where(causal, s, NEG)
            m_new = jnp.maximum(m, jnp.max(s, axis=0, keepdims=True))
            alpha = jnp.exp(m - m_new)
            p = jnp.exp(s - m_new)
            l = alpha * l + jnp.sum(p, axis=0, keepdims=True)
            vT = vT_ref[0, hh * HEAD_DIM:(hh + 1) * HEAD_DIM, pl.ds(off, tk)]
            acc = alpha * acc + jnp.dot(vT, p.astype(BF16), preferred_element_type=F32)
            return m_new, l, acc

        init = (jnp.full((1, tq), -jnp.inf, F32), jnp.zeros((1, tq), F32), jnp.zeros((HEAD_DIM, tq), F32))
        carry = lax.fori_loop(0, qi, lambda j, c: tile(j, c, False), init)
        m, l, acc = tile(qi, carry, True)
        o = acc * (1.0 / l)
        return o * lax.rsqrt(jnp.mean(o * o, axis=0, keepdims=True) + RMS_EPS)

    oT = jnp.concatenate([head(0), head(1)], axis=0)
    out_ref[0] = (oT.T * g_ref[0]).astype(out_ref.dtype)


def _attention(qT_aug, k_aug, vT, gain, *, tq):
    B, S, _ = k_aug.shape
    npair = N_ATT // 2
    return pl.pallas_call(
        functools.partial(_attn_kernel, tq=tq),
        grid=(B, npair, S // tq),
        in_specs=[
            pl.BlockSpec((1, 2 * AUG, tq), lambda b, p, i: (b, p, i)),
            pl.BlockSpec((1, S, 2 * AUG), lambda b, p, i: (b, 0, p), pipeline_mode=pl.Buffered(1)),
            pl.BlockSpec((1, 2 * HEAD_DIM, S), lambda b, p, i: (b, p, 0), pipeline_mode=pl.Buffered(1)),
            pl.BlockSpec((1, 1, 2 * HEAD_DIM), lambda b, p, i: (p, 0, 0)),
        ],
        out_specs=pl.BlockSpec((1, tq, 2 * HEAD_DIM), lambda b, p, i: (b, i, p)),
        out_shape=jax.ShapeDtypeStruct((B, S, D_ATT), BF16),
        compiler_params=pltpu.CompilerParams(
            dimension_semantics=("arbitrary", "arbitrary", "arbitrary"),
            vmem_limit_bytes=V7X_VMEM_LIMIT_BYTES),
        name="attention",
    )(qT_aug, k_aug, vT, gain)


def _lru_kernel(lx_ref, lg_ref, cw_ref, cb_ref, wr_ref, br_ref, wi_ref, bi_ref, lam_ref, g_ref, ones_ref,
                out_ref, tail_ref, h_ref, *, tl):
    @pl.when(pl.program_id(1) == 0)
    def _():
        tail_ref[...] = jnp.zeros_like(tail_ref)
        h_ref[...] = jnp.zeros_like(h_ref)

    x = lx_ref[0]
    xx = jnp.concatenate([tail_ref[...], x], axis=0)
    tail_ref[...] = x[tl - 8:tl, :]
    xc = cb_ref[...] + x * cw_ref[CONV_WIDTH - 1:CONV_WIDTH, :]
    for d in range(1, CONV_WIDTH):
        xc = xc + xx[8 - d:8 - d + tl, :] * cw_ref[CONV_WIDTH - 1 - d:CONV_WIDTH - d, :]

    xcb = xc.astype(BF16)
    r = jax.nn.sigmoid(jnp.dot(xcb, wr_ref[...], preferred_element_type=F32) + br_ref[...])
    i = jax.nn.sigmoid(jnp.dot(xcb, wi_ref[...], preferred_element_type=F32) + bi_ref[...])
    log_a = LRU_C * r * jax.nn.log_sigmoid(lam_ref[...])
    a = jnp.exp(log_a)
    u = jnp.sqrt(-jnp.tanh(log_a) * (a * a + 1.0)) * (i * xc)

    pos = lax.broadcasted_iota(jnp.int32, (tl, D_LRU), 0)
    d = 1
    while d < tl:
        keep = pos >= d
        a_prev = jnp.where(keep, pltpu.roll(a, d, 0), 1.0)
        u_prev = jnp.where(keep, pltpu.roll(u, d, 0), 0.0)
        u = a * u_prev + u
        a = a * a_prev
        d *= 2
    h = u + a * h_ref[0:1, :]
    h_ref[...] = jnp.broadcast_to(h[tl - 1:tl, :], h_ref.shape)

    y = jax.nn.gelu(lg_ref[0]) * h
    sq_hi, sq_mid, _ = _split3(y * y)
    ms = (jnp.dot(sq_hi, ones_ref[...], preferred_element_type=F32)
          + jnp.dot(sq_mid, ones_ref[...], preferred_element_type=F32)) * (1.0 / LRU_GROUP)
    out_ref[0] = (y * lax.rsqrt(ms + RMS_EPS) * g_ref[...]).astype(out_ref.dtype)


def _lru(lx, lg, cw, cb, wr, br, wi, bi, lam, gain, ones_bd, *, tl):
    B, S, _ = lx.shape
    small = [cw, cb, wr, br, wi, bi, lam, gain, ones_bd]
    return pl.pallas_call(
        functools.partial(_lru_kernel, tl=tl),
        grid=(B, S // tl),
        in_specs=[
            pl.BlockSpec((1, tl, D_LRU), lambda b, s: (b, s, 0)),
            pl.BlockSpec((1, tl, D_LRU), lambda b, s: (b, s, 0)),
        ] + [_const_spec(a.shape) for a in small],
        out_specs=pl.BlockSpec((1, tl, D_LRU), lambda b, s: (b, s, 0)),
        out_shape=jax.ShapeDtypeStruct((B, S, D_LRU), BF16),
        scratch_shapes=[pltpu.VMEM((8, D_LRU), F32), pltpu.VMEM((8, D_LRU), F32)],
        compiler_params=pltpu.CompilerParams(
            dimension_semantics=("arbitrary", "arbitrary"), vmem_limit_bytes=V7X_VMEM_LIMIT_BYTES),
        name="rglru",
    )(lx, lg, *small)


def _outproj_kernel(x_ref, ya_ref, yl_ref, wa_ref, wl_ref, g_ref, b_ref, out_ref, *, alpha):
    mix = jnp.dot(ya_ref[0], wa_ref[...], preferred_element_type=F32)
    mix = mix + jnp.dot(yl_ref[0], wl_ref[...], preferred_element_type=F32)
    out_ref[0] = _layer_norm(alpha * x_ref[0] + mix, g_ref[...], b_ref[...])


def _outproj(x, y_att, y_lru, wa, wl, g, b, *, ts, alpha):
    B, S, D = x.shape
    return pl.pallas_call(
        functools.partial(_outproj_kernel, alpha=alpha),
        grid=(B, S // ts),
        in_specs=[
            pl.BlockSpec((1, ts, D), lambda b_, s: (b_, s, 0)),
            pl.BlockSpec((1, ts, D_ATT), lambda b_, s: (b_, s, 0)),
            pl.BlockSpec((1, ts, D_LRU), lambda b_, s: (b_, s, 0)),
            _const_spec(wa.shape), _const_spec(wl.shape), _const_spec(g.shape), _const_spec(b.shape),
        ],
        out_specs=pl.BlockSpec((1, ts, D), lambda b_, s: (b_, s, 0)),
        out_shape=jax.ShapeDtypeStruct(x.shape, F32),
        compiler_params=pltpu.CompilerParams(
            dimension_semantics=("arbitrary", "arbitrary"), vmem_limit_bytes=V7X_VMEM_LIMIT_BYTES),
        name="outproj",
    )(x, y_att, y_lru, wa, wl, g, b)


def _ffn_kernel(x_ref, wg_ref, wu_ref, wd_ref, g_ref, b_ref, out_ref, acc_ref, *, alpha):
    f = pl.program_id(2)
    xb = x_ref[0].astype(BF16)
    gate = jnp.dot(xb, wg_ref[...], preferred_element_type=F32)
    up = jnp.dot(xb, wu_ref[...], preferred_element_type=F32)
    hid = (jax.nn.silu(gate) * up).astype(BF16)
    part = jnp.dot(hid, wd_ref[...], preferred_element_type=F32)

    @pl.when(f == 0)
    def _():
        acc_ref[...] = part

    @pl.when(f > 0)
    def _():
        acc_ref[...] += part

    @pl.when(f == pl.num_programs(2) - 1)
    def _():
        out_ref[0] = _layer_norm(alpha * x_ref[0] + acc_ref[...], g_ref[...], b_ref[...])


def _ffn(x, wg, wu, wd, g, b, *, ts, tf, alpha):
    B, S, D = x.shape
    d_ff = wg.shape[1]
    return pl.pallas_call(
        functools.partial(_ffn_kernel, alpha=alpha),
        grid=(B, S // ts, d_ff // tf),
        in_specs=[
            pl.BlockSpec((1, ts, D), lambda b_, s, f: (b_, s, 0)),
            pl.BlockSpec((D, tf), lambda b_, s, f: (0, f)),
            pl.BlockSpec((D, tf), lambda b_, s, f: (0, f)),
            pl.BlockSpec((tf, D), lambda b_, s, f: (f, 0)),
            _const_spec(g.shape), _const_spec(b.shape),
        ],
        out_specs=pl.BlockSpec((1, ts, D), lambda b_, s, f: (b_, s, 0)),
        out_shape=jax.ShapeDtypeStruct(x.shape, F32),
        scratch_shapes=[pltpu.VMEM((ts, D), F32)],
        compiler_params=pltpu.CompilerParams(
            dimension_semantics=("arbitrary", "arbitrary", "arbitrary"),
            vmem_limit_bytes=V7X_VMEM_LIMIT_BYTES),
        name="ffn",
    )(x, wg, wu, wd, g, b)


def _pad_heads(w):
    d = w.shape[0]
    w = w.reshape(d, -1, HEAD_DIM)
    return jnp.pad(w, ((0, 0), (0, 0), (0, AUG - HEAD_DIM))).reshape(d, -1)


def _layer_weights(w_in_l):
    sizes = [N_FOX * HEAD_DIM] * 3 + [N_FOX] + [N_MOBA * HEAD_DIM] * 3 + [D_LRU] * 2
    splits = [sum(sizes[:i + 1]) for i in range(len(sizes) - 1)]
    fq, fk, fv, ff, mq, mk, mv, lx, lg = jnp.split(w_in_l, splits, axis=1)
    wq = _pad_heads(jnp.concatenate([fq, mq], axis=1))
    wv = jnp.concatenate([fv, mv], axis=1)
    wt = jnp.concatenate([wq, wv], axis=1).T.astype(BF16)
    wk = _pad_heads(jnp.concatenate([fk, mk], axis=1))
    ffp = jnp.pad(ff, ((0, 0), (0, AUG - N_FOX)))
    wn = jnp.concatenate([wk, ffp, lx, lg], axis=1).astype(BF16)
    return wt, wn


def _block_diag(w):
    g, n, _ = w.shape
    eye = jnp.eye(g, dtype=w.dtype)
    return (eye[:, None, :, None] * w[:, :, None, :]).reshape(g * n, g * n)


def _fox_placement():
    place = jnp.zeros((N_SPLIT, AUG, N_FOX, AUG), F32)
    for c in range(N_SPLIT):
        for h in range(N_FOX):
            place = place.at[c, h, h, HEAD_DIM + c].set(1.0)
    return place.reshape(N_SPLIT * AUG, N_FOX * AUG).astype(BF16)


def _tiles(S, d_ff):
    def fit(n, want):
        t = min(n, want)
        while n % t:
            t //= 2
        return t

    nb = S // MOBA_BLOCK
    tf = d_ff // 2 if (d_ff // 2) % 128 == 0 else d_ff
    return dict(proj=fit(S, 256), fox=fit(S, 512), kmean=fit(nb, 8), attn=MOBA_BLOCK, lru=fit(S, 256),
                out=fit(S, 512), ffn=fit(S, 512), ffn_f=tf)


def kernel(x, w_in, b_fgate, conv_w, conv_b, w_rgate, b_rgate, w_igate, b_igate, lru_lambda, out_norm_g, w_out,
           ln1_g, ln1_b, w_ffn_gate, w_ffn_up, w_ffn_down, ln2_g, ln2_b):
    B, S, D = x.shape
    depth = w_in.shape[0]
    assert S % MOBA_BLOCK == 0 and S // MOBA_BLOCK <= AUG - HEAD_DIM
    alpha = (2 * depth) ** 0.25
    t = _tiles(S, w_ffn_gate.shape[-1])
    place = _fox_placement()
    ones_bd = _block_diag(jnp.ones((D_LRU // LRU_GROUP, LRU_GROUP, LRU_GROUP), BF16))
    row = lambda v: v.reshape(1, -1).astype(F32)

    for l in range(depth):
        wt, wn = _layer_weights(w_in[l])
        qT_aug, k_aug, vT, ff, lx, lg = _inproj(x, wt, wn, ts=t["proj"])

        bf_pad = jnp.pad(b_fgate[l], (0, AUG - N_FOX)).reshape(1, AUG)
        k_aug = _foxprep(ff, bf_pad, place, k_aug, ts=t["fox"])

        kmean = _kmean(k_aug, nblk=t["kmean"])
        qT_aug = _mobasel(kmean, qT_aug, tq=t["attn"])

        gain = out_norm_g[l]
        y_att = _attention(qT_aug, k_aug, vT, gain[:D_ATT].reshape(N_ATT // 2, 1, 2 * HEAD_DIM), tq=t["attn"])

        y_lru = _lru(lx, lg, conv_w[l], row(conv_b[l]), _block_diag(w_rgate[l]).astype(BF16), row(b_rgate[l]),
                     _block_diag(w_igate[l]).astype(BF16), row(b_igate[l]), row(lru_lambda[l]),
                     row(gain[D_ATT:]), ones_bd, tl=t["lru"])

        wo = w_out[l].astype(BF16)
        x = _outproj(x, y_att, y_lru, wo[:D_ATT], wo[D_ATT:], row(ln1_g[l]), row(ln1_b[l]),
                     ts=t["out"], alpha=alpha)
        x = _ffn(x, w_ffn_gate[l].astype(BF16), w_ffn_up[l].astype(BF16), w_ffn_down[l].astype(BF16),
                 row(ln2_g[l]), row(ln2_b[l]), ts=t["ffn"], tf=t["ffn_f"], alpha=alpha)
    return x
```

```python
import functools
import math

import jax
import jax.numpy as jnp
from jax import lax
from jax.experimental import pallas as pl
from jax.experimental.pallas import tpu as pltpu

F32 = jnp.float32
BF16 = jnp.bfloat16

HEAD_DIM = 64
AUG = 128
N_FOX = 6
N_MOBA = 6
N_ATT = N_FOX + N_MOBA
D_ATT = N_ATT * HEAD_DIM
D_LRU = 256
LRU_GROUP = 64
MOBA_BLOCK = 256
MOBA_TOPK = 3
CONV_WIDTH = 4
LRU_C = 8.0
LN_EPS = 1e-5
RMS_EPS = 1e-6
NEG = -1e30
LOG2E = math.log2(math.e)
QK_SCALE = LOG2E / math.sqrt(HEAD_DIM)
DENOM_ROWS = 16
N_SPLIT = 3
ATTN_HEADS_PER_STEP = 4

V7X_VMEM_LIMIT_BYTES = 48 * 1024 * 1024

NT_DIMS = (((1,), (1,)), ((), ()))


def _split3(x):
    hi = x.astype(BF16)
    r1 = x - hi.astype(F32)
    mid = r1.astype(BF16)
    lo = (r1 - mid.astype(F32)).astype(BF16)
    return hi, mid, lo


def _layer_norm(z, g, b):
    mu = jnp.mean(z, axis=-1, keepdims=True)
    zc = z - mu
    var = jnp.mean(zc * zc, axis=-1, keepdims=True)
    return zc * lax.rsqrt(var + LN_EPS) * g + b


def _const_spec(shape):
    nd = len(shape)
    return pl.BlockSpec(shape, lambda *_: (0,) * nd, pipeline_mode=pl.Buffered(1))


def _inproj_kernel(x_ref, wt_ref, wn_ref, qT_ref, k_ref, vT_ref, ff_ref, lx_ref, lg_ref, *, ts):
    si = pl.program_id(1)
    xb = x_ref[0].astype(BF16)
    t = lax.dot_general(wt_ref[...], xb, NT_DIMS, preferred_element_type=F32)
    row = lax.broadcasted_iota(jnp.int32, (AUG, ts), 0)
    ones_rows = (row >= HEAD_DIM) & (row < HEAD_DIM + N_SPLIT)
    for j in range(N_ATT):
        q = t[j * AUG:(j + 1) * AUG, :] * QK_SCALE
        if j < N_FOX:
            q = jnp.where(ones_rows, 1.0, q)
        qT_ref[0, j * AUG:(j + 1) * AUG, :] = q.astype(BF16)
    vT_ref[0] = t[N_ATT * AUG:, :].astype(BF16)

    n = jnp.dot(xb, wn_ref[...], preferred_element_type=F32)
    lane = lax.broadcasted_iota(jnp.int32, (ts, AUG), 1)
    blk = (si * ts + lax.broadcasted_iota(jnp.int32, (ts, AUG), 0)) // MOBA_BLOCK
    onehot = lane - HEAD_DIM == blk
    for j in range(N_ATT):
        k = n[:, j * AUG:(j + 1) * AUG]
        if j >= N_FOX:
            k = jnp.where(onehot, 1.0, k)
        k_ref[0, :, j * AUG:(j + 1) * AUG] = k.astype(BF16)
    base = N_ATT * AUG
    ff_ref[0] = n[:, base:base + AUG]
    lx_ref[0] = n[:, base + AUG:base + AUG + D_LRU]
    lg_ref[0] = n[:, base + AUG + D_LRU:base + AUG + 2 * D_LRU]


def _inproj(x, wt, wn, *, ts):
    B, S, D = x.shape
    grid = (B, S // ts)
    out_shape = (
        jax.ShapeDtypeStruct((B, N_ATT * AUG, S), BF16),
        jax.ShapeDtypeStruct((B, S, N_ATT * AUG), BF16),
        jax.ShapeDtypeStruct((B, D_ATT, S), BF16),
        jax.ShapeDtypeStruct((B, S, AUG), F32),
        jax.ShapeDtypeStruct((B, S, D_LRU), F32),
        jax.ShapeDtypeStruct((B, S, D_LRU), F32),
    )
    return pl.pallas_call(
        functools.partial(_inproj_kernel, ts=ts),
        grid=grid,
        in_specs=[
            pl.BlockSpec((1, ts, D), lambda b, s: (b, s, 0)),
            _const_spec(wt.shape),
            _const_spec(wn.shape),
        ],
        out_specs=(
            pl.BlockSpec((1, N_ATT * AUG, ts), lambda b, s: (b, 0, s)),
            pl.BlockSpec((1, ts, N_ATT * AUG), lambda b, s: (b, s, 0)),
            pl.BlockSpec((1, D_ATT, ts), lambda b, s: (b, 0, s)),
            pl.BlockSpec((1, ts, AUG), lambda b, s: (b, s, 0)),
            pl.BlockSpec((1, ts, D_LRU), lambda b, s: (b, s, 0)),
            pl.BlockSpec((1, ts, D_LRU), lambda b, s: (b, s, 0)),
        ),
        out_shape=out_shape,
        compiler_params=pltpu.CompilerParams(
            dimension_semantics=("arbitrary", "arbitrary"), vmem_limit_bytes=V7X_VMEM_LIMIT_BYTES),
        name="inproj",
    )(x, wt, wn)


def _foxprep_kernel(ff_ref, bf_ref, place_ref, k_ref, out_ref, tri_ref, carry_ref, *, ts):
    @pl.when(pl.program_id(1) == 0)
    def _():
        r = lax.broadcasted_iota(jnp.int32, (ts, ts), 0)
        c = lax.broadcasted_iota(jnp.int32, (ts, ts), 1)
        tri_ref[...] = (c <= r).astype(BF16)
        carry_ref[...] = jnp.zeros_like(carry_ref)

    log_f = jax.nn.log_sigmoid(ff_ref[0] + bf_ref[...])
    tri = tri_ref[...]
    cum = carry_ref[0:1, :]
    for piece in _split3(log_f):
        cum = cum + jnp.dot(tri, piece, preferred_element_type=F32)
    carry_ref[...] = jnp.broadcast_to(cum[ts - 1:ts, :], carry_ref.shape)

    pieces = jnp.concatenate(_split3(-LOG2E * cum), axis=1)
    bias = jnp.dot(pieces, place_ref[...], preferred_element_type=F32)
    lane = lax.broadcasted_iota(jnp.int32, (ts, N_FOX * AUG), 1) % AUG
    out_ref[0] = jnp.where(lane < HEAD_DIM, k_ref[0], bias.astype(BF16))


def _foxprep(ff, bf_pad, place, k_aug, *, ts):
    B, S, _ = ff.shape
    wf = N_FOX * AUG
    return pl.pallas_call(
        functools.partial(_foxprep_kernel, ts=ts),
        grid=(B, S // ts),
        in_specs=[
            pl.BlockSpec((1, ts, AUG), lambda b, s: (b, s, 0)),
            _const_spec(bf_pad.shape),
            _const_spec(place.shape),
            pl.BlockSpec((1, ts, wf), lambda b, s: (b, s, 0)),
        ],
        out_specs=pl.BlockSpec((1, ts, wf), lambda b, s: (b, s, 0)),
        out_shape=jax.ShapeDtypeStruct(k_aug.shape, k_aug.dtype),
        scratch_shapes=[pltpu.VMEM((ts, ts), BF16), pltpu.VMEM((8, AUG), F32)],
        input_output_aliases={3: 0},
        compiler_params=pltpu.CompilerParams(
            dimension_semantics=("arbitrary", "arbitrary"), vmem_limit_bytes=V7X_VMEM_LIMIT_BYTES),
        name="foxprep",
    )(ff, bf_pad, place, k_aug)


def _kmean_kernel(k_ref, out_ref, *, nblk):
    k = k_ref[0].astype(F32)
    out_ref[0] = jnp.mean(k.reshape(nblk, MOBA_BLOCK, k.shape[-1]), axis=1)


def _kmean(k_aug, *, nblk):
    B, S, _ = k_aug.shape
    nb = S // MOBA_BLOCK
    wm = N_MOBA * AUG
    return pl.pallas_call(
        functools.partial(_kmean_kernel, nblk=nblk),
        grid=(B, nb // nblk),
        in_specs=[pl.BlockSpec((1, nblk * MOBA_BLOCK, wm), lambda b, i: (b, i, 1))],
        out_specs=pl.BlockSpec((1, nblk, wm), lambda b, i: (b, i, 0)),
        out_shape=jax.ShapeDtypeStruct((B, nb, wm), F32),
        compiler_params=pltpu.CompilerParams(
            dimension_semantics=("arbitrary", "arbitrary"), vmem_limit_bytes=V7X_VMEM_LIMIT_BYTES),
        name="kmean",
    )(k_aug)


def _mobasel_kernel(kmean_ref, qT_ref, out_ref, *, nb, tq):
    cur = (pl.program_id(2) * tq) // MOBA_BLOCK
    lane = lax.broadcasted_iota(jnp.int32, (nb, AUG), 1)
    kmean = jnp.where(lane < HEAD_DIM, kmean_ref[0], 0.0).astype(BF16)
    qT = qT_ref[0]
    g = jnp.dot(kmean, qT, preferred_element_type=F32)
    blk = lax.broadcasted_iota(jnp.int32, (nb, tq), 0)
    past = blk < cur
    g = jnp.where(past, g, NEG)
    chosen = jnp.zeros((nb, tq), jnp.bool_)
    for _ in range(MOBA_TOPK):
        top = jnp.max(g, axis=0, keepdims=True)
        first = jnp.min(jnp.where(g == top, blk, nb), axis=0, keepdims=True)
        pick = blk == first
        chosen = chosen | pick
        g = jnp.where(pick, -jnp.inf, g)
    allowed = (chosen & past) | (blk == cur)
    out_ref[0, 0:HEAD_DIM, :] = qT[0:HEAD_DIM, :]
    bias = jnp.where(allowed, 0.0, NEG).astype(BF16)
    if nb < AUG - HEAD_DIM:
        bias = jnp.concatenate([bias, jnp.zeros((AUG - HEAD_DIM - nb, tq), BF16)], axis=0)
    out_ref[0, HEAD_DIM:AUG, :] = bias


def _mobasel(kmean, qT_aug, *, tq):
    B, nb, _ = kmean.shape
    S = qT_aug.shape[-1]
    return pl.pallas_call(
        functools.partial(_mobasel_kernel, nb=nb, tq=tq),
        grid=(B, N_MOBA, S // tq),
        in_specs=[
            pl.BlockSpec((1, nb, AUG), lambda b, h, i: (b, 0, h)),
            pl.BlockSpec((1, AUG, tq), lambda b, h, i: (b, N_FOX + h, i)),
        ],
        out_specs=pl.BlockSpec((1, AUG, tq), lambda b, h, i: (b, N_FOX + h, i)),
        out_shape=jax.ShapeDtypeStruct(qT_aug.shape, qT_aug.dtype),
        input_output_aliases={1: 0},
        compiler_params=pltpu.CompilerParams(
            dimension_semantics=("arbitrary", "arbitrary", "arbitrary"),
            vmem_limit_bytes=V7X_VMEM_LIMIT_BYTES),
        name="mobasel",
    )(kmean, qT_aug)


def _attn_kernel(qT_ref, k_ref, vT_ref, g_ref, out_ref, *, tq, kw, nh):
    qi = pl.program_id(2)
    heads = range(nh)

    def chunk(j, carry, masked):
        off = pl.multiple_of(j * kw, kw)
        scores = [jnp.dot(k_ref[0, pl.ds(off, kw), hh * AUG:(hh + 1) * AUG], qT_ref[0, hh * AUG:(hh + 1) * AUG, :],
                          preferred_element_type=F32) for hh in heads]
        if masked:
            kpos = off + lax.broadcasted_iota(jnp.int32, (kw, tq), 0)
            qpos = qi * tq + lax.broadcasted_iota(jnp.int32, (kw, tq), 1)
            causal = kpos <= qpos
        ones = jnp.ones((DENOM_ROWS, kw), BF16)
        out = []
        for hh in heads:
            m, acc = carry[hh]
            s = jnp.where(causal, scores[hh], NEG) if masked else scores[hh]
            m_new = jnp.maximum(m, jnp.max(s, axis=0, keepdims=True))
            alpha = jnp.exp2(m - m_new)
            p = jnp.exp2(s - m_new).astype(BF16)
            vT = vT_ref[0, hh * HEAD_DIM:(hh + 1) * HEAD_DIM, pl.ds(off, kw)]
            v_ones = jnp.concatenate([vT, ones], axis=0)
            acc = alpha * acc + jnp.dot(v_ones, p, preferred_element_type=F32)
            out.append((m_new, acc))
        return tuple(out)

    init = tuple((jnp.full((1, tq), -jnp.inf, F32), jnp.zeros((HEAD_DIM + DENOM_ROWS, tq), F32)) for _ in heads)
    n_past = (qi * tq) // kw
    carry = lax.fori_loop(0, n_past, lambda j, c: chunk(j, c, False), init)
    carry = chunk(n_past, carry, True)

    normed = []
    for _, acc in carry:
        o = acc[:HEAD_DIM] * (1.0 / acc[HEAD_DIM:HEAD_DIM + 1])
        normed.append(o * lax.rsqrt(jnp.mean(o * o, axis=0, keepdims=True) + RMS_EPS))
    oT = jnp.concatenate(normed, axis=0)
    out_ref[0] = (oT.T * g_ref[0]).astype(out_ref.dtype)


def _attention(qT_aug, k_aug, vT, gain, *, tq, kw, nh):
    B, S, _ = k_aug.shape
    return pl.pallas_call(
        functools.partial(_attn_kernel, tq=tq, kw=kw, nh=nh),
        grid=(B, N_ATT // nh, S // tq),
        in_specs=[
            pl.BlockSpec((1, nh * AUG, tq), lambda b, p, i: (b, p, i)),
            pl.BlockSpec((1, S, nh * AUG), lambda b, p, i: (b, 0, p), pipeline_mode=pl.Buffered(1)),
            pl.BlockSpec((1, nh * HEAD_DIM, S), lambda b, p, i: (b, p, 0), pipeline_mode=pl.Buffered(1)),
            pl.BlockSpec((1, 1, nh * HEAD_DIM), lambda b, p, i: (p, 0, 0)),
        ],
        out_specs=pl.BlockSpec((1, tq, nh * HEAD_DIM), lambda b, p, i: (b, i, p)),
        out_shape=jax.ShapeDtypeStruct((B, S, D_ATT), BF16),
        compiler_params=pltpu.CompilerParams(
            dimension_semantics=("arbitrary", "arbitrary", "arbitrary"),
            vmem_limit_bytes=V7X_VMEM_LIMIT_BYTES),
        name="attention",
    )(qT_aug, k_aug, vT, gain)


def _lru_kernel(lx_ref, lg_ref, cw_ref, cb_ref, wr_ref, br_ref, wi_ref, bi_ref, lam_ref, g_ref, ones_ref,
                out_ref, tail_ref, h_ref, *, tl):
    @pl.when(pl.program_id(1) == 0)
    def _():
        tail_ref[...] = jnp.zeros_like(tail_ref)
        h_ref[...] = jnp.zeros_like(h_ref)

    x = lx_ref[0]
    xx = jnp.concatenate([tail_ref[...], x], axis=0)
    tail_ref[...] = x[tl - 8:tl, :]
    xc = cb_ref[...] + x * cw_ref[CONV_WIDTH - 1:CONV_WIDTH, :]
    for d in range(1, CONV_WIDTH):
        xc = xc + xx[8 - d:8 - d + tl, :] * cw_ref[CONV_WIDTH - 1 - d:CONV_WIDTH - d, :]

    xcb = xc.astype(BF16)
    r = jax.nn.sigmoid(jnp.dot(xcb, wr_ref[...], preferred_element_type=F32) + br_ref[...])
    i = jax.nn.sigmoid(jnp.dot(xcb, wi_ref[...], preferred_element_type=F32) + bi_ref[...])
    log_a = LRU_C * r * jax.nn.log_sigmoid(lam_ref[...])
    a = jnp.exp(log_a)
    u = jnp.sqrt(-jnp.tanh(log_a) * (a * a + 1.0)) * (i * xc)

    pos = lax.broadcasted_iota(jnp.int32, (tl, D_LRU), 0)
    d = 1
    while d < tl:
        keep = pos >= d
        a_prev = jnp.where(keep, pltpu.roll(a, d, 0), 1.0)
        u_prev = jnp.where(keep, pltpu.roll(u, d, 0), 0.0)
        u = a * u_prev + u
        a = a * a_prev
        d *= 2
    h = u + a * h_ref[0:1, :]
    h_ref[...] = jnp.broadcast_to(h[tl - 1:tl, :], h_ref.shape)

    y = jax.nn.gelu(lg_ref[0]) * h
    sq_hi, sq_mid, _ = _split3(y * y)
    ms = (jnp.dot(sq_hi, ones_ref[...], preferred_element_type=F32)
          + jnp.dot(sq_mid, ones_ref[...], preferred_element_type=F32)) * (1.0 / LRU_GROUP)
    out_ref[0] = (y * lax.rsqrt(ms + RMS_EPS) * g_ref[...]).astype(out_ref.dtype)


def _lru(lx, lg, cw, cb, wr, br, wi, bi, lam, gain, ones_bd, *, tl):
    B, S, _ = lx.shape
    small = [cw, cb, wr, br, wi, bi, lam, gain, ones_bd]
    return pl.pallas_call(
        functools.partial(_lru_kernel, tl=tl),
        grid=(B, S // tl),
        in_specs=[
            pl.BlockSpec((1, tl, D_LRU), lambda b, s: (b, s, 0)),
            pl.BlockSpec((1, tl, D_LRU), lambda b, s: (b, s, 0)),
        ] + [_const_spec(a.shape) for a in small],
        out_specs=pl.BlockSpec((1, tl, D_LRU), lambda b, s: (b, s, 0)),
        out_shape=jax.ShapeDtypeStruct((B, S, D_LRU), BF16),
        scratch_shapes=[pltpu.VMEM((8, D_LRU), F32), pltpu.VMEM((8, D_LRU), F32)],
        compiler_params=pltpu.CompilerParams(
            dimension_semantics=("arbitrary", "arbitrary"), vmem_limit_bytes=V7X_VMEM_LIMIT_BYTES),
        name="rglru",
    )(lx, lg, *small)


def _outproj_kernel(x_ref, ya_ref, yl_ref, wa_ref, wl_ref, g_ref, b_ref, out_ref, *, alpha):
    mix = jnp.dot(ya_ref[0], wa_ref[...], preferred_element_type=F32)
    mix = mix + jnp.dot(yl_ref[0], wl_ref[...], preferred_element_type=F32)
    out_ref[0] = _layer_norm(alpha * x_ref[0] + mix, g_ref[...], b_ref[...])


def _outproj(x, y_att, y_lru, wa, wl, g, b, *, ts, alpha):
    B, S, D = x.shape
    return pl.pallas_call(
        functools.partial(_outproj_kernel, alpha=alpha),
        grid=(B, S // ts),
        in_specs=[
            pl.BlockSpec((1, ts, D), lambda b_, s: (b_, s, 0)),
            pl.BlockSpec((1, ts, D_ATT), lambda b_, s: (b_, s, 0)),
            pl.BlockSpec((1, ts, D_LRU), lambda b_, s: (b_, s, 0)),
            _const_spec(wa.shape), _const_spec(wl.shape), _const_spec(g.shape), _const_spec(b.shape),
        ],
        out_specs=pl.BlockSpec((1, ts, D), lambda b_, s: (b_, s, 0)),
        out_shape=jax.ShapeDtypeStruct(x.shape, F32),
        compiler_params=pltpu.CompilerParams(
            dimension_semantics=("arbitrary", "arbitrary"), vmem_limit_bytes=V7X_VMEM_LIMIT_BYTES),
        name="outproj",
    )(x, y_att, y_lru, wa, wl, g, b)


def _ffn_kernel(x_ref, wg_ref, wu_ref, wd_ref, g_ref, b_ref, out_ref, acc_ref, *, alpha):
    f = pl.program_id(2)
    xb = x_ref[0].astype(BF16)
    gate = jnp.dot(xb, wg_ref[...], preferred_element_type=F32)
    up = jnp.dot(xb, wu_ref[...], preferred_element_type=F32)
    hid = (jax.nn.silu(gate) * up).astype(BF16)
    part = jnp.dot(hid, wd_ref[...], preferred_element_type=F32)

    @pl.when(f == 0)
    def _():
        acc_ref[...] = part

    @pl.when(f > 0)
    def _():
        acc_ref[...] += part

    @pl.when(f == pl.num_programs(2) - 1)
    def _():
        out_ref[0] = _layer_norm(alpha * x_ref[0] + acc_ref[...], g_ref[...], b_ref[...])


def _ffn(x, wg, wu, wd, g, b, *, ts, tf, alpha):
    B, S, D = x.shape
    d_ff = wg.shape[1]
    return pl.pallas_call(
        functools.partial(_ffn_kernel, alpha=alpha),
        grid=(B, S // ts, d_ff // tf),
        in_specs=[
            pl.BlockSpec((1, ts, D), lambda b_, s, f: (b_, s, 0)),
            pl.BlockSpec((D, tf), lambda b_, s, f: (0, f)),
            pl.BlockSpec((D, tf), lambda b_, s, f: (0, f)),
            pl.BlockSpec((tf, D), lambda b_, s, f: (f, 0)),
            _const_spec(g.shape), _const_spec(b.shape),
        ],
        out_specs=pl.BlockSpec((1, ts, D), lambda b_, s, f: (b_, s, 0)),
        out_shape=jax.ShapeDtypeStruct(x.shape, F32),
        scratch_shapes=[pltpu.VMEM((ts, D), F32)],
        compiler_params=pltpu.CompilerParams(
            dimension_semantics=("arbitrary", "arbitrary", "arbitrary"),
            vmem_limit_bytes=V7X_VMEM_LIMIT_BYTES),
        name="ffn",
    )(x, wg, wu, wd, g, b)


def _pad_heads(w):
    d = w.shape[0]
    w = w.reshape(d, -1, HEAD_DIM)
    return jnp.pad(w, ((0, 0), (0, 0), (0, AUG - HEAD_DIM))).reshape(d, -1)


def _layer_weights(w_in_l):
    sizes = [N_FOX * HEAD_DIM] * 3 + [N_FOX] + [N_MOBA * HEAD_DIM] * 3 + [D_LRU] * 2
    splits = [sum(sizes[:i + 1]) for i in range(len(sizes) - 1)]
    fq, fk, fv, ff, mq, mk, mv, lx, lg = jnp.split(w_in_l, splits, axis=1)
    wq = _pad_heads(jnp.concatenate([fq, mq], axis=1))
    wv = jnp.concatenate([fv, mv], axis=1)
    wt = jnp.concatenate([wq, wv], axis=1).T.astype(BF16)
    wk = _pad_heads(jnp.concatenate([fk, mk], axis=1))
    ffp = jnp.pad(ff, ((0, 0), (0, AUG - N_FOX)))
    wn = jnp.concatenate([wk, ffp, lx, lg], axis=1).astype(BF16)
    return wt, wn


def _block_diag(w):
    g, n, _ = w.shape
    eye = jnp.eye(g, dtype=w.dtype)
    return (eye[:, None, :, None] * w[:, :, None, :]).reshape(g * n, g * n)


def _fox_placement():
    place = jnp.zeros((N_SPLIT, AUG, N_FOX, AUG), F32)
    for c in range(N_SPLIT):
        for h in range(N_FOX):
            place = place.at[c, h, h, HEAD_DIM + c].set(1.0)
    return place.reshape(N_SPLIT * AUG, N_FOX * AUG).astype(BF16)


def _tiles(S, d_ff):
    def fit(n, want):
        t = min(n, want)
        while n % t:
            t //= 2
        return t

    nb = S // MOBA_BLOCK
    tf = d_ff // 2 if (d_ff // 2) % 128 == 0 else d_ff
    return dict(proj=fit(S, 256), fox=fit(S, 512), kmean=fit(nb, 8), attn=fit(S, 512), lru=fit(S, 256),
                out=fit(S, 512), ffn=fit(S, 512), ffn_f=tf)


def kernel(x, w_in, b_fgate, conv_w, conv_b, w_rgate, b_rgate, w_igate, b_igate, lru_lambda, out_norm_g, w_out,
           ln1_g, ln1_b, w_ffn_gate, w_ffn_up, w_ffn_down, ln2_g, ln2_b):
    B, S, D = x.shape
    depth = w_in.shape[0]
    assert S % MOBA_BLOCK == 0 and S // MOBA_BLOCK <= AUG - HEAD_DIM
    alpha = (2 * depth) ** 0.25
    t = _tiles(S, w_ffn_gate.shape[-1])
    place = _fox_placement()
    ones_bd = _block_diag(jnp.ones((D_LRU // LRU_GROUP, LRU_GROUP, LRU_GROUP), BF16))
    row = lambda v: v.reshape(1, -1).astype(F32)

    for l in range(depth):
        wt, wn = _layer_weights(w_in[l])
        qT_aug, k_aug, vT, ff, lx, lg = _inproj(x, wt, wn, ts=t["proj"])

        bf_pad = jnp.pad(b_fgate[l], (0, AUG - N_FOX)).reshape(1, AUG)
        k_aug = _foxprep(ff, bf_pad, place, k_aug, ts=t["fox"])

        kmean = _kmean(k_aug, nblk=t["kmean"])
        qT_aug = _mobasel(kmean, qT_aug, tq=MOBA_BLOCK)

        gain = out_norm_g[l]
        nh = ATTN_HEADS_PER_STEP
        y_att = _attention(qT_aug, k_aug, vT, gain[:D_ATT].reshape(N_ATT // nh, 1, nh * HEAD_DIM),
                           tq=t["attn"], kw=t["attn"], nh=nh)

        y_lru = _lru(lx, lg, conv_w[l], row(conv_b[l]), _block_diag(w_rgate[l]).astype(BF16), row(b_rgate[l]),
                     _block_diag(w_igate[l]).astype(BF16), row(b_igate[l]), row(lru_lambda[l]),
                     row(gain[D_ATT:]), ones_bd, tl=t["lru"])

        wo = w_out[l].astype(BF16)
        x = _outproj(x, y_att, y_lru, wo[:D_ATT], wo[D_ATT:], row(ln1_g[l]), row(ln1_b[l]),
                     ts=t["out"], alpha=alpha)
        x = _ffn(x, w_ffn_gate[l].astype(BF16), w_ffn_up[l].astype(BF16), w_ffn_down[l].astype(BF16),
                 row(ln2_g[l]), row(ln2_b[l]), ts=t["ffn"], tf=t["ffn_f"], alpha=alpha)
    return x
```

```python
import functools
import math

import jax
import jax.numpy as jnp
from jax import lax
from jax.experimental import pallas as pl
from jax.experimental.pallas import tpu as pltpu

F32 = jnp.float32
BF16 = jnp.bfloat16

HEAD_DIM = 64
AUG = 128
N_FOX = 6
N_MOBA = 6
N_ATT = N_FOX + N_MOBA
D_ATT = N_ATT * HEAD_DIM
D_LRU = 256
LRU_GROUP = 64
MOBA_BLOCK = 256
MOBA_TOPK = 3
CONV_WIDTH = 4
LRU_C = 8.0
LN_EPS = 1e-5
RMS_EPS = 1e-6
NEG = -1e30
LOG2E = math.log2(math.e)
QK_SCALE = LOG2E / math.sqrt(HEAD_DIM)
DENOM_ROWS = 16
N_SPLIT = 3
ATTN_HEADS_PER_STEP = 4
ATTN_ISSUE_ORDER = "SUSUSUSU"

V7X_VMEM_LIMIT_BYTES = 48 * 1024 * 1024

NT_DIMS = (((1,), (1,)), ((), ()))


def _split3(x):
    hi = x.astype(BF16)
    r1 = x - hi.astype(F32)
    mid = r1.astype(BF16)
    lo = (r1 - mid.astype(F32)).astype(BF16)
    return hi, mid, lo


def _layer_norm(z, g, b):
    mu = jnp.mean(z, axis=-1, keepdims=True)
    zc = z - mu
    var = jnp.mean(zc * zc, axis=-1, keepdims=True)
    return zc * lax.rsqrt(var + LN_EPS) * g + b


def _const_spec(shape):
    nd = len(shape)
    return pl.BlockSpec(shape, lambda *_: (0,) * nd, pipeline_mode=pl.Buffered(1))


def _inproj_kernel(x_ref, wt_ref, wn_ref, qT_ref, k_ref, vT_ref, ff_ref, lx_ref, lg_ref, *, ts):
    si = pl.program_id(1)
    xb = x_ref[0].astype(BF16)
    t = lax.dot_general(wt_ref[...], xb, NT_DIMS, preferred_element_type=F32)
    row = lax.broadcasted_iota(jnp.int32, (AUG, ts), 0)
    ones_rows = (row >= HEAD_DIM) & (row < HEAD_DIM + N_SPLIT)
    for j in range(N_ATT):
        q = t[j * AUG:(j + 1) * AUG, :] * QK_SCALE
        if j < N_FOX:
            q = jnp.where(ones_rows, 1.0, q)
        qT_ref[0, j * AUG:(j + 1) * AUG, :] = q.astype(BF16)
    vT_ref[0] = t[N_ATT * AUG:, :].astype(BF16)

    n = jnp.dot(xb, wn_ref[...], preferred_element_type=F32)
    lane = lax.broadcasted_iota(jnp.int32, (ts, AUG), 1)
    blk = (si * ts + lax.broadcasted_iota(jnp.int32, (ts, AUG), 0)) // MOBA_BLOCK
    onehot = lane - HEAD_DIM == blk
    for j in range(N_ATT):
        k = n[:, j * AUG:(j + 1) * AUG]
        if j >= N_FOX:
            k = jnp.where(onehot, 1.0, k)
        k_ref[0, :, j * AUG:(j + 1) * AUG] = k.astype(BF16)
    base = N_ATT * AUG
    ff_ref[0] = n[:, base:base + AUG]
    lx_ref[0] = n[:, base + AUG:base + AUG + D_LRU]
    lg_ref[0] = n[:, base + AUG + D_LRU:base + AUG + 2 * D_LRU]


def _inproj(x, wt, wn, *, ts):
    B, S, D = x.shape
    grid = (B, S // ts)
    out_shape = (
        jax.ShapeDtypeStruct((B, N_ATT * AUG, S), BF16),
        jax.ShapeDtypeStruct((B, S, N_ATT * AUG), BF16),
        jax.ShapeDtypeStruct((B, D_ATT, S), BF16),
        jax.ShapeDtypeStruct((B, S, AUG), F32),
        jax.ShapeDtypeStruct((B, S, D_LRU), F32),
        jax.ShapeDtypeStruct((B, S, D_LRU), F32),
    )
    return pl.pallas_call(
        functools.partial(_inproj_kernel, ts=ts),
        grid=grid,
        in_specs=[
            pl.BlockSpec((1, ts, D), lambda b, s: (b, s, 0)),
            _const_spec(wt.shape),
            _const_spec(wn.shape),
        ],
        out_specs=(
            pl.BlockSpec((1, N_ATT * AUG, ts), lambda b, s: (b, 0, s)),
            pl.BlockSpec((1, ts, N_ATT * AUG), lambda b, s: (b, s, 0)),
            pl.BlockSpec((1, D_ATT, ts), lambda b, s: (b, 0, s)),
            pl.BlockSpec((1, ts, AUG), lambda b, s: (b, s, 0)),
            pl.BlockSpec((1, ts, D_LRU), lambda b, s: (b, s, 0)),
            pl.BlockSpec((1, ts, D_LRU), lambda b, s: (b, s, 0)),
        ),
        out_shape=out_shape,
        compiler_params=pltpu.CompilerParams(
            dimension_semantics=("arbitrary", "arbitrary"), vmem_limit_bytes=V7X_VMEM_LIMIT_BYTES),
        name="inproj",
    )(x, wt, wn)


def _foxprep_kernel(ff_ref, bf_ref, place_ref, k_ref, out_ref, tri_ref, carry_ref, *, ts):
    @pl.when(pl.program_id(1) == 0)
    def _():
        r = lax.broadcasted_iota(jnp.int32, (ts, ts), 0)
        c = lax.broadcasted_iota(jnp.int32, (ts, ts), 1)
        tri_ref[...] = (c <= r).astype(BF16)
        carry_ref[...] = jnp.zeros_like(carry_ref)

    log_f = jax.nn.log_sigmoid(ff_ref[0] + bf_ref[...])
    tri = tri_ref[...]
    cum = carry_ref[0:1, :]
    for piece in _split3(log_f):
        cum = cum + jnp.dot(tri, piece, preferred_element_type=F32)
    carry_ref[...] = jnp.broadcast_to(cum[ts - 1:ts, :], carry_ref.shape)

    pieces = jnp.concatenate(_split3(-LOG2E * cum), axis=1)
    bias = jnp.dot(pieces, place_ref[...], preferred_element_type=F32)
    lane = lax.broadcasted_iota(jnp.int32, (ts, N_FOX * AUG), 1) % AUG
    out_ref[0] = jnp.where(lane < HEAD_DIM, k_ref[0], bias.astype(BF16))


def _foxprep(ff, bf_pad, place, k_aug, *, ts):
    B, S, _ = ff.shape
    wf = N_FOX * AUG
    return pl.pallas_call(
        functools.partial(_foxprep_kernel, ts=ts),
        grid=(B, S // ts),
        in_specs=[
            pl.BlockSpec((1, ts, AUG), lambda b, s: (b, s, 0)),
            _const_spec(bf_pad.shape),
            _const_spec(place.shape),
            pl.BlockSpec((1, ts, wf), lambda b, s: (b, s, 0)),
        ],
        out_specs=pl.BlockSpec((1, ts, wf), lambda b, s: (b, s, 0)),
        out_shape=jax.ShapeDtypeStruct(k_aug.shape, k_aug.dtype),
        scratch_shapes=[pltpu.VMEM((ts, ts), BF16), pltpu.VMEM((8, AUG), F32)],
        input_output_aliases={3: 0},
        compiler_params=pltpu.CompilerParams(
            dimension_semantics=("arbitrary", "arbitrary"), vmem_limit_bytes=V7X_VMEM_LIMIT_BYTES),
        name="foxprep",
    )(ff, bf_pad, place, k_aug)


def _kmean_kernel(k_ref, out_ref, *, nblk):
    k = k_ref[0].astype(F32)
    out_ref[0] = jnp.mean(k.reshape(nblk, MOBA_BLOCK, k.shape[-1]), axis=1)


def _kmean(k_aug, *, nblk):
    B, S, _ = k_aug.shape
    nb = S // MOBA_BLOCK
    wm = N_MOBA * AUG
    return pl.pallas_call(
        functools.partial(_kmean_kernel, nblk=nblk),
        grid=(B, nb // nblk),
        in_specs=[pl.BlockSpec((1, nblk * MOBA_BLOCK, wm), lambda b, i: (b, i, 1))],
        out_specs=pl.BlockSpec((1, nblk, wm), lambda b, i: (b, i, 0)),
        out_shape=jax.ShapeDtypeStruct((B, nb, wm), F32),
        compiler_params=pltpu.CompilerParams(
            dimension_semantics=("arbitrary", "arbitrary"), vmem_limit_bytes=V7X_VMEM_LIMIT_BYTES),
        name="kmean",
    )(k_aug)


def _mobasel_kernel(kmean_ref, qT_ref, out_ref, *, nb, tq):
    cur = (pl.program_id(2) * tq + lax.broadcasted_iota(jnp.int32, (nb, tq), 1)) // MOBA_BLOCK
    lane = lax.broadcasted_iota(jnp.int32, (nb, AUG), 1)
    kmean = jnp.where(lane < HEAD_DIM, kmean_ref[0], 0.0).astype(BF16)
    qT = qT_ref[0]
    g = jnp.dot(kmean, qT, preferred_element_type=F32)
    blk = lax.broadcasted_iota(jnp.int32, (nb, tq), 0)
    past = blk < cur
    g = jnp.where(past, g, NEG)
    chosen = jnp.zeros((nb, tq), jnp.bool_)
    for _ in range(MOBA_TOPK):
        top = jnp.max(g, axis=0, keepdims=True)
        first = jnp.min(jnp.where(g == top, blk, nb), axis=0, keepdims=True)
        pick = blk == first
        chosen = chosen | pick
        g = jnp.where(pick, -jnp.inf, g)
    allowed = (chosen & past) | (blk == cur)
    out_ref[0, 0:HEAD_DIM, :] = qT[0:HEAD_DIM, :]
    bias = jnp.where(allowed, 0.0, NEG).astype(BF16)
    if nb < AUG - HEAD_DIM:
        bias = jnp.concatenate([bias, jnp.zeros((AUG - HEAD_DIM - nb, tq), BF16)], axis=0)
    out_ref[0, HEAD_DIM:AUG, :] = bias


def _mobasel(kmean, qT_aug, *, tq):
    B, nb, _ = kmean.shape
    S = qT_aug.shape[-1]
    return pl.pallas_call(
        functools.partial(_mobasel_kernel, nb=nb, tq=tq),
        grid=(B, N_MOBA, S // tq),
        in_specs=[
            pl.BlockSpec((1, nb, AUG), lambda b, h, i: (b, 0, h)),
            pl.BlockSpec((1, AUG, tq), lambda b, h, i: (b, N_FOX + h, i)),
        ],
        out_specs=pl.BlockSpec((1, AUG, tq), lambda b, h, i: (b, N_FOX + h, i)),
        out_shape=jax.ShapeDtypeStruct(qT_aug.shape, qT_aug.dtype),
        input_output_aliases={1: 0},
        compiler_params=pltpu.CompilerParams(
            dimension_semantics=("arbitrary", "arbitrary", "arbitrary"),
            vmem_limit_bytes=V7X_VMEM_LIMIT_BYTES),
        name="mobasel",
    )(kmean, qT_aug)


def _attn_kernel(qT_ref, k_ref, vT_ref, g_ref, out_ref, sa_ref, sb_ref, mxa_ref, mxb_ref, *, tq, kw, nh):
    qi = pl.program_id(2)
    heads = range(nh)
    ones = jnp.ones((DENOM_ROWS, kw), BF16)

    def score_head(c, hh, s_ref, mx_ref):
        off = pl.multiple_of(c * kw, kw)
        s = jnp.dot(k_ref[0, pl.ds(off, kw), hh * AUG:(hh + 1) * AUG], qT_ref[0, hh * AUG:(hh + 1) * AUG, :],
                    preferred_element_type=F32)
        s_ref[hh] = s
        mx_ref[hh] = jnp.max(s, axis=0, keepdims=True)

    def update_head(c, hh, s_ref, mx_ref, state, masked):
        off = pl.multiple_of(c * kw, kw)
        m, acc = state
        if masked:
            kpos = off + lax.broadcasted_iota(jnp.int32, (kw, tq), 0)
            qpos = qi * tq + lax.broadcasted_iota(jnp.int32, (kw, tq), 1)
            s = jnp.where(kpos <= qpos, s_ref[hh], NEG)
            mx = jnp.max(s, axis=0, keepdims=True)
        else:
            s = s_ref[hh]
            mx = mx_ref[hh]
        m_new = jnp.maximum(m, mx)
        alpha = jnp.exp2(m - m_new)
        p = jnp.exp2(s - m_new).astype(BF16)
        vT = vT_ref[0, hh * HEAD_DIM:(hh + 1) * HEAD_DIM, pl.ds(off, kw)]
        v_ones = jnp.concatenate([vT, ones], axis=0)
        return m_new, alpha * acc + jnp.dot(v_ones, p, preferred_element_type=F32)

    def stage(c_next, next_bufs, c_cur, cur_bufs, carry, masked=False):
        carry = list(carry)
        todo = {"S": iter(heads), "U": iter(heads)}
        for step in ATTN_ISSUE_ORDER:
            hh = next(todo[step])
            if step == "S":
                if c_next is not None:
                    score_head(c_next, hh, *next_bufs)
            else:
                carry[hh] = update_head(c_cur, hh, *cur_bufs, carry[hh], masked)
        return tuple(carry)

    buf_a, buf_b = (sa_ref, mxa_ref), (sb_ref, mxb_ref)

    def past_pair(i, carry):
        carry = stage(2 * i + 1, buf_b, 2 * i, buf_a, carry)
        return stage(2 * i + 2, buf_a, 2 * i + 1, buf_b, carry)

    init = tuple((jnp.full((1, tq), -jnp.inf, F32), jnp.zeros((HEAD_DIM + DENOM_ROWS, tq), F32)) for _ in heads)
    for hh in heads:
        score_head(0, hh, *buf_a)
    carry = lax.fori_loop(0, qi, past_pair, init)
    carry = stage(2 * qi + 1, buf_b, 2 * qi, buf_a, carry, masked=True)
    carry = stage(None, None, 2 * qi + 1, buf_b, carry, masked=True)

    normed = []
    for _, acc in carry:
        o = acc[:HEAD_DIM] * (1.0 / acc[HEAD_DIM:HEAD_DIM + 1])
        normed.append(o * lax.rsqrt(jnp.mean(o * o, axis=0, keepdims=True) + RMS_EPS))
    oT = jnp.concatenate(normed, axis=0)
    out_ref[0] = (oT.T * g_ref[0]).astype(out_ref.dtype)


def _attention(qT_aug, k_aug, vT, gain, *, tq, nh):
    B, S, _ = k_aug.shape
    kw = tq // 2
    score_buf = pltpu.VMEM((nh, kw, tq), F32)
    max_buf = pltpu.VMEM((nh, 1, tq), F32)
    return pl.pallas_call(
        functools.partial(_attn_kernel, tq=tq, kw=kw, nh=nh),
        grid=(B, N_ATT // nh, S // tq),
        in_specs=[
            pl.BlockSpec((1, nh * AUG, tq), lambda b, p, i: (b, p, i)),
            pl.BlockSpec((1, S, nh * AUG), lambda b, p, i: (b, 0, p), pipeline_mode=pl.Buffered(1)),
            pl.BlockSpec((1, nh * HEAD_DIM, S), lambda b, p, i: (b, p, 0), pipeline_mode=pl.Buffered(1)),
            pl.BlockSpec((1, 1, nh * HEAD_DIM), lambda b, p, i: (p, 0, 0)),
        ],
        out_specs=pl.BlockSpec((1, tq, nh * HEAD_DIM), lambda b, p, i: (b, i, p)),
        out_shape=jax.ShapeDtypeStruct((B, S, D_ATT), BF16),
        scratch_shapes=[score_buf, score_buf, max_buf, max_buf],
        compiler_params=pltpu.CompilerParams(
            dimension_semantics=("arbitrary", "arbitrary", "arbitrary"),
            vmem_limit_bytes=V7X_VMEM_LIMIT_BYTES),
        name="attention",
    )(qT_aug, k_aug, vT, gain)


def _lru_kernel(lx_ref, lg_ref, cw_ref, cb_ref, wr_ref, br_ref, wi_ref, bi_ref, lam_ref, g_ref, ones_ref,
                out_ref, tail_ref, h_ref, *, tl):
    @pl.when(pl.program_id(1) == 0)
    def _():
        tail_ref[...] = jnp.zeros_like(tail_ref)
        h_ref[...] = jnp.zeros_like(h_ref)

    x = lx_ref[0]
    xx = jnp.concatenate([tail_ref[...], x], axis=0)
    tail_ref[...] = x[tl - 8:tl, :]
    xc = cb_ref[...] + x * cw_ref[CONV_WIDTH - 1:CONV_WIDTH, :]
    for d in range(1, CONV_WIDTH):
        xc = xc + xx[8 - d:8 - d + tl, :] * cw_ref[CONV_WIDTH - 1 - d:CONV_WIDTH - d, :]

    xcb = xc.astype(BF16)
    r = jax.nn.sigmoid(jnp.dot(xcb, wr_ref[...], preferred_element_type=F32) + br_ref[...])
    i = jax.nn.sigmoid(jnp.dot(xcb, wi_ref[...], preferred_element_type=F32) + bi_ref[...])
    log_a = LRU_C * r * jax.nn.log_sigmoid(lam_ref[...])
    a = jnp.exp(log_a)
    u = jnp.sqrt(-jnp.tanh(log_a) * (a * a + 1.0)) * (i * xc)

    pos = lax.broadcasted_iota(jnp.int32, (tl, D_LRU), 0)
    d = 1
    while d < tl:
        keep = pos >= d
        a_prev = jnp.where(keep, pltpu.roll(a, d, 0), 1.0)
        u_prev = jnp.where(keep, pltpu.roll(u, d, 0), 0.0)
        u = a * u_prev + u
        a = a * a_prev
        d *= 2
    h = u + a * h_ref[0:1, :]
    h_ref[...] = jnp.broadcast_to(h[tl - 1:tl, :], h_ref.shape)

    y = jax.nn.gelu(lg_ref[0]) * h
    sq_hi, sq_mid, _ = _split3(y * y)
    ms = (jnp.dot(sq_hi, ones_ref[...], preferred_element_type=F32)
          + jnp.dot(sq_mid, ones_ref[...], preferred_element_type=F32)) * (1.0 / LRU_GROUP)
    out_ref[0] = (y * lax.rsqrt(ms + RMS_EPS) * g_ref[...]).astype(out_ref.dtype)


def _lru(lx, lg, cw, cb, wr, br, wi, bi, lam, gain, ones_bd, *, tl):
    B, S, _ = lx.shape
    small = [cw, cb, wr, br, wi, bi, lam, gain, ones_bd]
    return pl.pallas_call(
        functools.partial(_lru_kernel, tl=tl),
        grid=(B, S // tl),
        in_specs=[
            pl.BlockSpec((1, tl, D_LRU), lambda b, s: (b, s, 0)),
            pl.BlockSpec((1, tl, D_LRU), lambda b, s: (b, s, 0)),
        ] + [_const_spec(a.shape) for a in small],
        out_specs=pl.BlockSpec((1, tl, D_LRU), lambda b, s: (b, s, 0)),
        out_shape=jax.ShapeDtypeStruct((B, S, D_LRU), BF16),
        scratch_shapes=[pltpu.VMEM((8, D_LRU), F32), pltpu.VMEM((8, D_LRU), F32)],
        compiler_params=pltpu.CompilerParams(
            dimension_semantics=("arbitrary", "arbitrary"), vmem_limit_bytes=V7X_VMEM_LIMIT_BYTES),
        name="rglru",
    )(lx, lg, *small)


def _outproj_kernel(x_ref, ya_ref, yl_ref, wa_ref, wl_ref, g_ref, b_ref, out_ref, *, alpha):
    mix = jnp.dot(ya_ref[0], wa_ref[...], preferred_element_type=F32)
    mix = mix + jnp.dot(yl_ref[0], wl_ref[...], preferred_element_type=F32)
    out_ref[0] = _layer_norm(alpha * x_ref[0] + mix, g_ref[...], b_ref[...])


def _outproj(x, y_att, y_lru, wa, wl, g, b, *, ts, alpha):
    B, S, D = x.shape
    return pl.pallas_call(
        functools.partial(_outproj_kernel, alpha=alpha),
        grid=(B, S // ts),
        in_specs=[
            pl.BlockSpec((1, ts, D), lambda b_, s: (b_, s, 0)),
            pl.BlockSpec((1, ts, D_ATT), lambda b_, s: (b_, s, 0)),
            pl.BlockSpec((1, ts, D_LRU), lambda b_, s: (b_, s, 0)),
            _const_spec(wa.shape), _const_spec(wl.shape), _const_spec(g.shape), _const_spec(b.shape),
        ],
        out_specs=pl.BlockSpec((1, ts, D), lambda b_, s: (b_, s, 0)),
        out_shape=jax.ShapeDtypeStruct(x.shape, F32),
        compiler_params=pltpu.CompilerParams(
            dimension_semantics=("arbitrary", "arbitrary"), vmem_limit_bytes=V7X_VMEM_LIMIT_BYTES),
        name="outproj",
    )(x, y_att, y_lru, wa, wl, g, b)


def _ffn_kernel(x_ref, wg_ref, wu_ref, wd_ref, g_ref, b_ref, out_ref, acc_ref, *, alpha):
    f = pl.program_id(2)
    xb = x_ref[0].astype(BF16)
    gate = jnp.dot(xb, wg_ref[...], preferred_element_type=F32)
    up = jnp.dot(xb, wu_ref[...], preferred_element_type=F32)
    hid = (jax.nn.silu(gate) * up).astype(BF16)
    part = jnp.dot(hid, wd_ref[...], preferred_element_type=F32)

    @pl.when(f == 0)
    def _():
        acc_ref[...] = part

    @pl.when(f > 0)
    def _():
        acc_ref[...] += part

    @pl.when(f == pl.num_programs(2) - 1)
    def _():
        out_ref[0] = _layer_norm(alpha * x_ref[0] + acc_ref[...], g_ref[...], b_ref[...])


def _ffn(x, wg, wu, wd, g, b, *, ts, tf, alpha):
    B, S, D = x.shape
    d_ff = wg.shape[1]
    return pl.pallas_call(
        functools.partial(_ffn_kernel, alpha=alpha),
        grid=(B, S // ts, d_ff // tf),
        in_specs=[
            pl.BlockSpec((1, ts, D), lambda b_, s, f: (b_, s, 0)),
            pl.BlockSpec((D, tf), lambda b_, s, f: (0, f)),
            pl.BlockSpec((D, tf), lambda b_, s, f: (0, f)),
            pl.BlockSpec((tf, D), lambda b_, s, f: (f, 0)),
            _const_spec(g.shape), _const_spec(b.shape),
        ],
        out_specs=pl.BlockSpec((1, ts, D), lambda b_, s, f: (b_, s, 0)),
        out_shape=jax.ShapeDtypeStruct(x.shape, F32),
        scratch_shapes=[pltpu.VMEM((ts, D), F32)],
        compiler_params=pltpu.CompilerParams(
            dimension_semantics=("arbitrary", "arbitrary", "arbitrary"),
            vmem_limit_bytes=V7X_VMEM_LIMIT_BYTES),
        name="ffn",
    )(x, wg, wu, wd, g, b)


def _pad_heads(w):
    d = w.shape[0]
    w = w.reshape(d, -1, HEAD_DIM)
    return jnp.pad(w, ((0, 0), (0, 0), (0, AUG - HEAD_DIM))).reshape(d, -1)


def _layer_weights(w_in_l):
    sizes = [N_FOX * HEAD_DIM] * 3 + [N_FOX] + [N_MOBA * HEAD_DIM] * 3 + [D_LRU] * 2
    splits = [sum(sizes[:i + 1]) for i in range(len(sizes) - 1)]
    fq, fk, fv, ff, mq, mk, mv, lx, lg = jnp.split(w_in_l, splits, axis=1)
    wq = _pad_heads(jnp.concatenate([fq, mq], axis=1))
    wv = jnp.concatenate([fv, mv], axis=1)
    wt = jnp.concatenate([wq, wv], axis=1).T.astype(BF16)
    wk = _pad_heads(jnp.concatenate([fk, mk], axis=1))
    ffp = jnp.pad(ff, ((0, 0), (0, AUG - N_FOX)))
    wn = jnp.concatenate([wk, ffp, lx, lg], axis=1).astype(BF16)
    return wt, wn


def _block_diag(w):
    g, n, _ = w.shape
    eye = jnp.eye(g, dtype=w.dtype)
    return (eye[:, None, :, None] * w[:, :, None, :]).reshape(g * n, g * n)


def _fox_placement():
    place = jnp.zeros((N_SPLIT, AUG, N_FOX, AUG), F32)
    for c in range(N_SPLIT):
        for h in range(N_FOX):
            place = place.at[c, h, h, HEAD_DIM + c].set(1.0)
    return place.reshape(N_SPLIT * AUG, N_FOX * AUG).astype(BF16)


def _tiles(S, d_ff):
    def fit(n, want):
        t = min(n, want)
        while n % t:
            t //= 2
        return t

    nb = S // MOBA_BLOCK
    tf = d_ff // 2 if (d_ff // 2) % 128 == 0 else d_ff
    return dict(proj=fit(S, 256), fox=fit(S, 512), kmean=fit(nb, 8), sel=fit(S, 2048), attn=fit(S, 512), lru=fit(S, 256),
                out=fit(S, 512), ffn=fit(S, 512), ffn_f=tf)


def kernel(x, w_in, b_fgate, conv_w, conv_b, w_rgate, b_rgate, w_igate, b_igate, lru_lambda, out_norm_g, w_out,
           ln1_g, ln1_b, w_ffn_gate, w_ffn_up, w_ffn_down, ln2_g, ln2_b):
    B, S, D = x.shape
    depth = w_in.shape[0]
    assert S % MOBA_BLOCK == 0 and S // MOBA_BLOCK <= AUG - HEAD_DIM
    alpha = (2 * depth) ** 0.25
    t = _tiles(S, w_ffn_gate.shape[-1])
    place = _fox_placement()
    ones_bd = _block_diag(jnp.ones((D_LRU // LRU_GROUP, LRU_GROUP, LRU_GROUP), BF16))
    row = lambda v: v.reshape(1, -1).astype(F32)

    for l in range(depth):
        wt, wn = _layer_weights(w_in[l])
        qT_aug, k_aug, vT, ff, lx, lg = _inproj(x, wt, wn, ts=t["proj"])

        bf_pad = jnp.pad(b_fgate[l], (0, AUG - N_FOX)).reshape(1, AUG)
        k_aug = _foxprep(ff, bf_pad, place, k_aug, ts=t["fox"])

        kmean = _kmean(k_aug, nblk=t["kmean"])
        qT_aug = _mobasel(kmean, qT_aug, tq=t["sel"])

        gain = out_norm_g[l]
        nh = ATTN_HEADS_PER_STEP
        y_att = _attention(qT_aug, k_aug, vT, gain[:D_ATT].reshape(N_ATT // nh, 1, nh * HEAD_DIM),
                           tq=t["attn"], nh=nh)

        y_lru = _lru(lx, lg, conv_w[l], row(conv_b[l]), _block_diag(w_rgate[l]).astype(BF16), row(b_rgate[l]),
                     _block_diag(w_igate[l]).astype(BF16), row(b_igate[l]), row(lru_lambda[l]),
                     row(gain[D_ATT:]), ones_bd, tl=t["lru"])

        wo = w_out[l].astype(BF16)
        x = _outproj(x, y_att, y_lru, wo[:D_ATT], wo[D_ATT:], row(ln1_g[l]), row(ln1_b[l]),
                     ts=t["out"], alpha=alpha)
        x = _ffn(x, w_ffn_gate[l].astype(BF16), w_ffn_up[l].astype(BF16), w_ffn_down[l].astype(BF16),
                 row(ln2_g[l]), row(ln2_b[l]), ts=t["ffn"], tf=t["ffn_f"], alpha=alpha)
    return x
```

```python
import functools
import math

import jax
import jax.numpy as jnp
from jax import lax
from jax.experimental import pallas as pl
from jax.experimental.pallas import tpu as pltpu

F32 = jnp.float32
BF16 = jnp.bfloat16

HEAD_DIM = 64
AUG = 128
N_FOX = 6
N_MOBA = 6
N_ATT = N_FOX + N_MOBA
D_ATT = N_ATT * HEAD_DIM
D_LRU = 256
LRU_GROUP = 64
MOBA_BLOCK = 256
MOBA_TOPK = 3
CONV_WIDTH = 4
LRU_C = 8.0
LN_EPS = 1e-5
RMS_EPS = 1e-6
NEG = -1e30
LOG2E = math.log2(math.e)
QK_SCALE = LOG2E / math.sqrt(HEAD_DIM)
DENOM_ROWS = 16
N_SPLIT = 3
ATTN_HEADS_PER_STEP = 4
ATTN_CHUNKS_PER_ITER = 4
ATTN_ISSUE_ORDER = "SUSUSUSU"

V7X_VMEM_LIMIT_BYTES = 48 * 1024 * 1024

NT_DIMS = (((1,), (1,)), ((), ()))


def _split3(x):
    hi = x.astype(BF16)
    r1 = x - hi.astype(F32)
    mid = r1.astype(BF16)
    lo = (r1 - mid.astype(F32)).astype(BF16)
    return hi, mid, lo


def _layer_norm(z, g, b):
    mu = jnp.mean(z, axis=-1, keepdims=True)
    zc = z - mu
    var = jnp.mean(zc * zc, axis=-1, keepdims=True)
    return zc * lax.rsqrt(var + LN_EPS) * g + b


def _const_spec(shape):
    nd = len(shape)
    return pl.BlockSpec(shape, lambda *_: (0,) * nd, pipeline_mode=pl.Buffered(1))


def _inproj_kernel(x_ref, wt_ref, wn_ref, qT_ref, k_ref, vT_ref, ff_ref, lx_ref, lg_ref, *, ts):
    si = pl.program_id(1)
    xb = x_ref[0].astype(BF16)
    t = lax.dot_general(wt_ref[...], xb, NT_DIMS, preferred_element_type=F32)
    row = lax.broadcasted_iota(jnp.int32, (AUG, ts), 0)
    ones_rows = (row >= HEAD_DIM) & (row < HEAD_DIM + N_SPLIT)
    for j in range(N_ATT):
        q = t[j * AUG:(j + 1) * AUG, :] * QK_SCALE
        if j < N_FOX:
            q = jnp.where(ones_rows, 1.0, q)
        qT_ref[0, j * AUG:(j + 1) * AUG, :] = q.astype(BF16)
    vT_ref[0] = t[N_ATT * AUG:, :].astype(BF16)

    n = jnp.dot(xb, wn_ref[...], preferred_element_type=F32)
    lane = lax.broadcasted_iota(jnp.int32, (ts, AUG), 1)
    blk = (si * ts + lax.broadcasted_iota(jnp.int32, (ts, AUG), 0)) // MOBA_BLOCK
    onehot = lane - HEAD_DIM == blk
    for j in range(N_ATT):
        k = n[:, j * AUG:(j + 1) * AUG]
        if j >= N_FOX:
            k = jnp.where(onehot, 1.0, k)
        k_ref[0, :, j * AUG:(j + 1) * AUG] = k.astype(BF16)
    base = N_ATT * AUG
    ff_ref[0] = n[:, base:base + AUG]
    lx_ref[0] = n[:, base + AUG:base + AUG + D_LRU]
    lg_ref[0] = n[:, base + AUG + D_LRU:base + AUG + 2 * D_LRU]


def _inproj(x, wt, wn, *, ts):
    B, S, D = x.shape
    grid = (B, S // ts)
    out_shape = (
        jax.ShapeDtypeStruct((B, N_ATT * AUG, S), BF16),
        jax.ShapeDtypeStruct((B, S, N_ATT * AUG), BF16),
        jax.ShapeDtypeStruct((B, D_ATT, S), BF16),
        jax.ShapeDtypeStruct((B, S, AUG), F32),
        jax.ShapeDtypeStruct((B, S, D_LRU), F32),
        jax.ShapeDtypeStruct((B, S, D_LRU), F32),
    )
    return pl.pallas_call(
        functools.partial(_inproj_kernel, ts=ts),
        grid=grid,
        in_specs=[
            pl.BlockSpec((1, ts, D), lambda b, s: (b, s, 0)),
            _const_spec(wt.shape),
            _const_spec(wn.shape),
        ],
        out_specs=(
            pl.BlockSpec((1, N_ATT * AUG, ts), lambda b, s: (b, 0, s)),
            pl.BlockSpec((1, ts, N_ATT * AUG), lambda b, s: (b, s, 0)),
            pl.BlockSpec((1, D_ATT, ts), lambda b, s: (b, 0, s)),
            pl.BlockSpec((1, ts, AUG), lambda b, s: (b, s, 0)),
            pl.BlockSpec((1, ts, D_LRU), lambda b, s: (b, s, 0)),
            pl.BlockSpec((1, ts, D_LRU), lambda b, s: (b, s, 0)),
        ),
        out_shape=out_shape,
        compiler_params=pltpu.CompilerParams(
            dimension_semantics=("arbitrary", "arbitrary"), vmem_limit_bytes=V7X_VMEM_LIMIT_BYTES),
        name="inproj",
    )(x, wt, wn)


def _foxprep_kernel(ff_ref, bf_ref, place_ref, k_ref, out_ref, tri_ref, carry_ref, *, ts):
    @pl.when(pl.program_id(1) == 0)
    def _():
        r = lax.broadcasted_iota(jnp.int32, (ts, ts), 0)
        c = lax.broadcasted_iota(jnp.int32, (ts, ts), 1)
        tri_ref[...] = (c <= r).astype(BF16)
        carry_ref[...] = jnp.zeros_like(carry_ref)

    log_f = jax.nn.log_sigmoid(ff_ref[0] + bf_ref[...])
    tri = tri_ref[...]
    cum = carry_ref[0:1, :]
    for piece in _split3(log_f):
        cum = cum + jnp.dot(tri, piece, preferred_element_type=F32)
    carry_ref[...] = jnp.broadcast_to(cum[ts - 1:ts, :], carry_ref.shape)

    pieces = jnp.concatenate(_split3(-LOG2E * cum), axis=1)
    bias = jnp.dot(pieces, place_ref[...], preferred_element_type=F32)
    lane = lax.broadcasted_iota(jnp.int32, (ts, N_FOX * AUG), 1) % AUG
    out_ref[0] = jnp.where(lane < HEAD_DIM, k_ref[0], bias.astype(BF16))


def _foxprep(ff, bf_pad, place, k_aug, *, ts):
    B, S, _ = ff.shape
    wf = N_FOX * AUG
    return pl.pallas_call(
        functools.partial(_foxprep_kernel, ts=ts),
        grid=(B, S // ts),
        in_specs=[
            pl.BlockSpec((1, ts, AUG), lambda b, s: (b, s, 0)),
            _const_spec(bf_pad.shape),
            _const_spec(place.shape),
            pl.BlockSpec((1, ts, wf), lambda b, s: (b, s, 0)),
        ],
        out_specs=pl.BlockSpec((1, ts, wf), lambda b, s: (b, s, 0)),
        out_shape=jax.ShapeDtypeStruct(k_aug.shape, k_aug.dtype),
        scratch_shapes=[pltpu.VMEM((ts, ts), BF16), pltpu.VMEM((8, AUG), F32)],
        input_output_aliases={3: 0},
        compiler_params=pltpu.CompilerParams(
            dimension_semantics=("arbitrary", "arbitrary"), vmem_limit_bytes=V7X_VMEM_LIMIT_BYTES),
        name="foxprep",
    )(ff, bf_pad, place, k_aug)


def _kmean_kernel(k_ref, out_ref, *, nblk):
    k = k_ref[0].astype(F32)
    out_ref[0] = jnp.mean(k.reshape(nblk, MOBA_BLOCK, k.shape[-1]), axis=1)


def _kmean(k_aug, *, nblk):
    B, S, _ = k_aug.shape
    nb = S // MOBA_BLOCK
    wm = N_MOBA * AUG
    return pl.pallas_call(
        functools.partial(_kmean_kernel, nblk=nblk),
        grid=(B, nb // nblk),
        in_specs=[pl.BlockSpec((1, nblk * MOBA_BLOCK, wm), lambda b, i: (b, i, 1))],
        out_specs=pl.BlockSpec((1, nblk, wm), lambda b, i: (b, i, 0)),
        out_shape=jax.ShapeDtypeStruct((B, nb, wm), F32),
        compiler_params=pltpu.CompilerParams(
            dimension_semantics=("arbitrary", "arbitrary"), vmem_limit_bytes=V7X_VMEM_LIMIT_BYTES),
        name="kmean",
    )(k_aug)


def _mobasel_kernel(kmean_ref, qT_ref, out_ref, *, nb, tq):
    cur = (pl.program_id(2) * tq + lax.broadcasted_iota(jnp.int32, (nb, tq), 1)) // MOBA_BLOCK
    lane = lax.broadcasted_iota(jnp.int32, (nb, AUG), 1)
    kmean = jnp.where(lane < HEAD_DIM, kmean_ref[0], 0.0).astype(BF16)
    qT = qT_ref[0]
    g = jnp.dot(kmean, qT, preferred_element_type=F32)
    blk = lax.broadcasted_iota(jnp.int32, (nb, tq), 0)
    past = blk < cur
    g = jnp.where(past, g, NEG)
    chosen = jnp.zeros((nb, tq), jnp.bool_)
    for _ in range(MOBA_TOPK):
        top = jnp.max(g, axis=0, keepdims=True)
        first = jnp.min(jnp.where(g == top, blk, nb), axis=0, keepdims=True)
        pick = blk == first
        chosen = chosen | pick
        g = jnp.where(pick, -jnp.inf, g)
    allowed = (chosen & past) | (blk == cur)
    out_ref[0, 0:HEAD_DIM, :] = qT[0:HEAD_DIM, :]
    bias = jnp.where(allowed, 0.0, NEG).astype(BF16)
    if nb < AUG - HEAD_DIM:
        bias = jnp.concatenate([bias, jnp.zeros((AUG - HEAD_DIM - nb, tq), BF16)], axis=0)
    out_ref[0, HEAD_DIM:AUG, :] = bias


def _mobasel(kmean, qT_aug, *, tq):
    B, nb, _ = kmean.shape
    S = qT_aug.shape[-1]
    return pl.pallas_call(
        functools.partial(_mobasel_kernel, nb=nb, tq=tq),
        grid=(B, N_MOBA, S // tq),
        in_specs=[
            pl.BlockSpec((1, nb, AUG), lambda b, h, i: (b, 0, h)),
            pl.BlockSpec((1, AUG, tq), lambda b, h, i: (b, N_FOX + h, i)),
        ],
        out_specs=pl.BlockSpec((1, AUG, tq), lambda b, h, i: (b, N_FOX + h, i)),
        out_shape=jax.ShapeDtypeStruct(qT_aug.shape, qT_aug.dtype),
        input_output_aliases={1: 0},
        compiler_params=pltpu.CompilerParams(
            dimension_semantics=("arbitrary", "arbitrary", "arbitrary"),
            vmem_limit_bytes=V7X_VMEM_LIMIT_BYTES),
        name="mobasel",
    )(kmean, qT_aug)


def _attn_kernel(qT_ref, k_ref, vT_ref, g_ref, out_ref, sa_ref, sb_ref, mxa_ref, mxb_ref, m_ref, acc_ref,
                 *, tq, kw, nh):
    qi = pl.program_id(2)
    heads = range(nh)
    ones = jnp.ones((DENOM_ROWS, kw), BF16)
    bufs = ((sa_ref, mxa_ref), (sb_ref, mxb_ref))

    def score_head(c, hh, s_ref, mx_ref):
        off = pl.multiple_of(c * kw, kw)
        s = jnp.dot(k_ref[0, pl.ds(off, kw), hh * AUG:(hh + 1) * AUG], qT_ref[0, hh * AUG:(hh + 1) * AUG, :],
                    preferred_element_type=F32)
        s_ref[hh] = s
        mx_ref[hh] = jnp.max(s, axis=0, keepdims=True)

    def update_head(c, hh, s_ref, mx_ref, masked):
        off = pl.multiple_of(c * kw, kw)
        if masked:
            kpos = off + lax.broadcasted_iota(jnp.int32, (kw, tq), 0)
            qpos = qi * tq + lax.broadcasted_iota(jnp.int32, (kw, tq), 1)
            s = jnp.where(kpos <= qpos, s_ref[hh], NEG)
            mx = jnp.max(s, axis=0, keepdims=True)
        else:
            s = s_ref[hh]
            mx = mx_ref[hh]
        m = m_ref[hh]
        m_new = jnp.maximum(m, mx)
        alpha = jnp.exp2(m - m_new)
        p = jnp.exp2(s - m_new).astype(BF16)
        vT = vT_ref[0, hh * HEAD_DIM:(hh + 1) * HEAD_DIM, pl.ds(off, kw)]
        v_ones = jnp.concatenate([vT, ones], axis=0)
        acc_ref[hh] = alpha * acc_ref[hh] + jnp.dot(v_ones, p, preferred_element_type=F32)
        m_ref[hh] = m_new

    def stage(c, parity, *, prefetch=True, masked=False):
        todo = {"S": iter(heads), "U": iter(heads)}
        for step in ATTN_ISSUE_ORDER:
            hh = next(todo[step])
            if step == "U":
                update_head(c, hh, *bufs[parity], masked)
            elif prefetch:
                score_head(c + 1, hh, *bufs[1 - parity])

    def past_chunks(first, count):
        for t in range(count):
            stage(first + t, t % 2)

    m_ref[...] = jnp.full(m_ref.shape, -jnp.inf, F32)
    acc_ref[...] = jnp.zeros(acc_ref.shape, F32)
    for hh in heads:
        score_head(0, hh, *bufs[0])

    pairs_per_iter = ATTN_CHUNKS_PER_ITER // 2

    @pl.loop(0, qi // pairs_per_iter)
    def _(j):
        past_chunks(j * ATTN_CHUNKS_PER_ITER, ATTN_CHUNKS_PER_ITER)

    for r in range(1, pairs_per_iter):
        @pl.when(qi % pairs_per_iter == r)
        def _():
            past_chunks((qi - r) * 2, 2 * r)

    stage(2 * qi, 0, masked=True)
    stage(2 * qi + 1, 1, prefetch=False, masked=True)

    normed = []
    for hh in heads:
        acc = acc_ref[hh]
        o = acc[:HEAD_DIM] * (1.0 / acc[HEAD_DIM:HEAD_DIM + 1])
        normed.append(o * lax.rsqrt(jnp.mean(o * o, axis=0, keepdims=True) + RMS_EPS))
    oT = jnp.concatenate(normed, axis=0)
    out_ref[0] = (oT.T * g_ref[0]).astype(out_ref.dtype)


def _attention(qT_aug, k_aug, vT, gain, *, tq, nh):
    B, S, _ = k_aug.shape
    kw = tq // 2
    score_buf = pltpu.VMEM((nh, kw, tq), F32)
    max_buf = pltpu.VMEM((nh, 1, tq), F32)
    acc_buf = pltpu.VMEM((nh, HEAD_DIM + DENOM_ROWS, tq), F32)
    return pl.pallas_call(
        functools.partial(_attn_kernel, tq=tq, kw=kw, nh=nh),
        grid=(B, N_ATT // nh, S // tq),
        in_specs=[
            pl.BlockSpec((1, nh * AUG, tq), lambda b, p, i: (b, p, i)),
            pl.BlockSpec((1, S, nh * AUG), lambda b, p, i: (b, 0, p), pipeline_mode=pl.Buffered(1)),
            pl.BlockSpec((1, nh * HEAD_DIM, S), lambda b, p, i: (b, p, 0), pipeline_mode=pl.Buffered(1)),
            pl.BlockSpec((1, 1, nh * HEAD_DIM), lambda b, p, i: (p, 0, 0)),
        ],
        out_specs=pl.BlockSpec((1, tq, nh * HEAD_DIM), lambda b, p, i: (b, i, p)),
        out_shape=jax.ShapeDtypeStruct((B, S, D_ATT), BF16),
        scratch_shapes=[score_buf, score_buf, max_buf, max_buf, max_buf, acc_buf],
        compiler_params=pltpu.CompilerParams(
            dimension_semantics=("arbitrary", "arbitrary", "arbitrary"),
            vmem_limit_bytes=V7X_VMEM_LIMIT_BYTES),
        name="attention",
    )(qT_aug, k_aug, vT, gain)


def _lru_kernel(lx_ref, lg_ref, cw_ref, cb_ref, wr_ref, br_ref, wi_ref, bi_ref, lam_ref, g_ref, ones_ref,
                out_ref, tail_ref, h_ref, *, tl):
    @pl.when(pl.program_id(1) == 0)
    def _():
        tail_ref[...] = jnp.zeros_like(tail_ref)
        h_ref[...] = jnp.zeros_like(h_ref)

    x = lx_ref[0]
    xx = jnp.concatenate([tail_ref[...], x], axis=0)
    tail_ref[...] = x[tl - 8:tl, :]
    xc = cb_ref[...] + x * cw_ref[CONV_WIDTH - 1:CONV_WIDTH, :]
    for d in range(1, CONV_WIDTH):
        xc = xc + xx[8 - d:8 - d + tl, :] * cw_ref[CONV_WIDTH - 1 - d:CONV_WIDTH - d, :]

    xcb = xc.astype(BF16)
    r = jax.nn.sigmoid(jnp.dot(xcb, wr_ref[...], preferred_element_type=F32) + br_ref[...])
    i = jax.nn.sigmoid(jnp.dot(xcb, wi_ref[...], preferred_element_type=F32) + bi_ref[...])
    log_a = LRU_C * r * jax.nn.log_sigmoid(lam_ref[...])
    a = jnp.exp(log_a)
    u = jnp.sqrt(-jnp.tanh(log_a) * (a * a + 1.0)) * (i * xc)

    pos = lax.broadcasted_iota(jnp.int32, (tl, D_LRU), 0)
    d = 1
    while d < tl:
        keep = pos >= d
        a_prev = jnp.where(keep, pltpu.roll(a, d, 0), 1.0)
        u_prev = jnp.where(keep, pltpu.roll(u, d, 0), 0.0)
        u = a * u_prev + u
        a = a * a_prev
        d *= 2
    h = u + a * h_ref[0:1, :]
    h_ref[...] = jnp.broadcast_to(h[tl - 1:tl, :], h_ref.shape)

    y = jax.nn.gelu(lg_ref[0]) * h
    sq_hi, sq_mid, _ = _split3(y * y)
    ms = (jnp.dot(sq_hi, ones_ref[...], preferred_element_type=F32)
          + jnp.dot(sq_mid, ones_ref[...], preferred_element_type=F32)) * (1.0 / LRU_GROUP)
    out_ref[0] = (y * lax.rsqrt(ms + RMS_EPS) * g_ref[...]).astype(out_ref.dtype)


def _lru(lx, lg, cw, cb, wr, br, wi, bi, lam, gain, ones_bd, *, tl):
    B, S, _ = lx.shape
    small = [cw, cb, wr, br, wi, bi, lam, gain, ones_bd]
    return pl.pallas_call(
        functools.partial(_lru_kernel, tl=tl),
        grid=(B, S // tl),
        in_specs=[
            pl.BlockSpec((1, tl, D_LRU), lambda b, s: (b, s, 0)),
            pl.BlockSpec((1, tl, D_LRU), lambda b, s: (b, s, 0)),
        ] + [_const_spec(a.shape) for a in small],
        out_specs=pl.BlockSpec((1, tl, D_LRU), lambda b, s: (b, s, 0)),
        out_shape=jax.ShapeDtypeStruct((B, S, D_LRU), BF16),
        scratch_shapes=[pltpu.VMEM((8, D_LRU), F32), pltpu.VMEM((8, D_LRU), F32)],
        compiler_params=pltpu.CompilerParams(
            dimension_semantics=("arbitrary", "arbitrary"), vmem_limit_bytes=V7X_VMEM_LIMIT_BYTES),
        name="rglru",
    )(lx, lg, *small)


def _outproj_kernel(x_ref, ya_ref, yl_ref, wa_ref, wl_ref, g_ref, b_ref, out_ref, *, alpha):
    mix = jnp.dot(ya_ref[0], wa_ref[...], preferred_element_type=F32)
    mix = mix + jnp.dot(yl_ref[0], wl_ref[...], preferred_element_type=F32)
    out_ref[0] = _layer_norm(alpha * x_ref[0] + mix, g_ref[...], b_ref[...])


def _outproj(x, y_att, y_lru, wa, wl, g, b, *, ts, alpha):
    B, S, D = x.shape
    return pl.pallas_call(
        functools.partial(_outproj_kernel, alpha=alpha),
        grid=(B, S // ts),
        in_specs=[
            pl.BlockSpec((1, ts, D), lambda b_, s: (b_, s, 0)),
            pl.BlockSpec((1, ts, D_ATT), lambda b_, s: (b_, s, 0)),
            pl.BlockSpec((1, ts, D_LRU), lambda b_, s: (b_, s, 0)),
            _const_spec(wa.shape), _const_spec(wl.shape), _const_spec(g.shape), _const_spec(b.shape),
        ],
        out_specs=pl.BlockSpec((1, ts, D), lambda b_, s: (b_, s, 0)),
        out_shape=jax.ShapeDtypeStruct(x.shape, F32),
        compiler_params=pltpu.CompilerParams(
            dimension_semantics=("arbitrary", "arbitrary"), vmem_limit_bytes=V7X_VMEM_LIMIT_BYTES),
        name="outproj",
    )(x, y_att, y_lru, wa, wl, g, b)


def _ffn_kernel(x_ref, wg_ref, wu_ref, wd_ref, g_ref, b_ref, out_ref, acc_ref, *, alpha):
    f = pl.program_id(2)
    xb = x_ref[0].astype(BF16)
    gate = jnp.dot(xb, wg_ref[...], preferred_element_type=F32)
    up = jnp.dot(xb, wu_ref[...], preferred_element_type=F32)
    hid = (jax.nn.silu(gate) * up).astype(BF16)
    part = jnp.dot(hid, wd_ref[...], preferred_element_type=F32)

    @pl.when(f == 0)
    def _():
        acc_ref[...] = part

    @pl.when(f > 0)
    def _():
        acc_ref[...] += part

    @pl.when(f == pl.num_programs(2) - 1)
    def _():
        out_ref[0] = _layer_norm(alpha * x_ref[0] + acc_ref[...], g_ref[...], b_ref[...])


def _ffn(x, wg, wu, wd, g, b, *, ts, tf, alpha):
    B, S, D = x.shape
    d_ff = wg.shape[1]
    return pl.pallas_call(
        functools.partial(_ffn_kernel, alpha=alpha),
        grid=(B, S // ts, d_ff // tf),
        in_specs=[
            pl.BlockSpec((1, ts, D), lambda b_, s, f: (b_, s, 0)),
            pl.BlockSpec((D, tf), lambda b_, s, f: (0, f)),
            pl.BlockSpec((D, tf), lambda b_, s, f: (0, f)),
            pl.BlockSpec((tf, D), lambda b_, s, f: (f, 0)),
            _const_spec(g.shape), _const_spec(b.shape),
        ],
        out_specs=pl.BlockSpec((1, ts, D), lambda b_, s, f: (b_, s, 0)),
        out_shape=jax.ShapeDtypeStruct(x.shape, F32),
        scratch_shapes=[pltpu.VMEM((ts, D), F32)],
        compiler_params=pltpu.CompilerParams(
            dimension_semantics=("arbitrary", "arbitrary", "arbitrary"),
            vmem_limit_bytes=V7X_VMEM_LIMIT_BYTES),
        name="ffn",
    )(x, wg, wu, wd, g, b)


def _pad_heads(w):
    d = w.shape[0]
    w = w.reshape(d, -1, HEAD_DIM)
    return jnp.pad(w, ((0, 0), (0, 0), (0, AUG - HEAD_DIM))).reshape(d, -1)


def _layer_weights(w_in_l):
    sizes = [N_FOX * HEAD_DIM] * 3 + [N_FOX] + [N_MOBA * HEAD_DIM] * 3 + [D_LRU] * 2
    splits = [sum(sizes[:i + 1]) for i in range(len(sizes) - 1)]
    fq, fk, fv, ff, mq, mk, mv, lx, lg = jnp.split(w_in_l, splits, axis=1)
    wq = _pad_heads(jnp.concatenate([fq, mq], axis=1))
    wv = jnp.concatenate([fv, mv], axis=1)
    wt = jnp.concatenate([wq, wv], axis=1).T.astype(BF16)
    wk = _pad_heads(jnp.concatenate([fk, mk], axis=1))
    ffp = jnp.pad(ff, ((0, 0), (0, AUG - N_FOX)))
    wn = jnp.concatenate([wk, ffp, lx, lg], axis=1).astype(BF16)
    return wt, wn


def _block_diag(w):
    g, n, _ = w.shape
    eye = jnp.eye(g, dtype=w.dtype)
    return (eye[:, None, :, None] * w[:, :, None, :]).reshape(g * n, g * n)


def _fox_placement():
    place = jnp.zeros((N_SPLIT, AUG, N_FOX, AUG), F32)
    for c in range(N_SPLIT):
        for h in range(N_FOX):
            place = place.at[c, h, h, HEAD_DIM + c].set(1.0)
    return place.reshape(N_SPLIT * AUG, N_FOX * AUG).astype(BF16)


def _tiles(S, d_ff):
    def fit(n, want):
        t = min(n, want)
        while n % t:
            t //= 2
        return t

    nb = S // MOBA_BLOCK
    tf = d_ff // 2 if (d_ff // 2) % 128 == 0 else d_ff
    return dict(proj=fit(S, 256), fox=fit(S, 512), kmean=fit(nb, 8), sel=fit(S, 2048), attn=fit(S, 512), lru=fit(S, 256),
                out=fit(S, 512), ffn=fit(S, 512), ffn_f=tf)


def kernel(x, w_in, b_fgate, conv_w, conv_b, w_rgate, b_rgate, w_igate, b_igate, lru_lambda, out_norm_g, w_out,
           ln1_g, ln1_b, w_ffn_gate, w_ffn_up, w_ffn_down, ln2_g, ln2_b):
    B, S, D = x.shape
    depth = w_in.shape[0]
    assert S % MOBA_BLOCK == 0 and S // MOBA_BLOCK <= AUG - HEAD_DIM
    alpha = (2 * depth) ** 0.25
    t = _tiles(S, w_ffn_gate.shape[-1])
    place = _fox_placement()
    ones_bd = _block_diag(jnp.ones((D_LRU // LRU_GROUP, LRU_GROUP, LRU_GROUP), BF16))
    row = lambda v: v.reshape(1, -1).astype(F32)

    for l in range(depth):
        wt, wn = _layer_weights(w_in[l])
        qT_aug, k_aug, vT, ff, lx, lg = _inproj(x, wt, wn, ts=t["proj"])

        bf_pad = jnp.pad(b_fgate[l], (0, AUG - N_FOX)).reshape(1, AUG)
        k_aug = _foxprep(ff, bf_pad, place, k_aug, ts=t["fox"])

        kmean = _kmean(k_aug, nblk=t["kmean"])
        qT_aug = _mobasel(kmean, qT_aug, tq=t["sel"])

        gain = out_norm_g[l]
        nh = ATTN_HEADS_PER_STEP
        y_att = _attention(qT_aug, k_aug, vT, gain[:D_ATT].reshape(N_ATT // nh, 1, nh * HEAD_DIM),
                           tq=t["attn"], nh=nh)

        y_lru = _lru(lx, lg, conv_w[l], row(conv_b[l]), _block_diag(w_rgate[l]).astype(BF16), row(b_rgate[l]),
                     _block_diag(w_igate[l]).astype(BF16), row(b_igate[l]), row(lru_lambda[l]),
                     row(gain[D_ATT:]), ones_bd, tl=t["lru"])

        wo = w_out[l].astype(BF16)
        x = _outproj(x, y_att, y_lru, wo[:D_ATT], wo[D_ATT:], row(ln1_g[l]), row(ln1_b[l]),
                     ts=t["out"], alpha=alpha)
        x = _ffn(x, w_ffn_gate[l].astype(BF16), w_ffn_up[l].astype(BF16), w_ffn_down[l].astype(BF16),
                 row(ln2_g[l]), row(ln2_b[l]), ts=t["ffn"], tf=t["ffn_f"], alpha=alpha)
    return x
```

```python
import functools
import math

import jax
import jax.numpy as jnp
from jax import lax
from jax.experimental import pallas as pl
from jax.experimental.pallas import tpu as pltpu

F32 = jnp.float32
BF16 = jnp.bfloat16

HEAD_DIM = 64
AUG = 128
N_FOX = 6
N_MOBA = 6
N_ATT = N_FOX + N_MOBA
D_ATT = N_ATT * HEAD_DIM
D_LRU = 256
LRU_GROUP = 64
MOBA_BLOCK = 256
MOBA_TOPK = 3
CONV_WIDTH = 4
LRU_C = 8.0
LN_EPS = 1e-5
RMS_EPS = 1e-6
NEG = -1e30
LOG2E = math.log2(math.e)
QK_SCALE = LOG2E / math.sqrt(HEAD_DIM)
DENOM_ROWS = 16
N_SPLIT = 3
ATTN_HEADS_PER_STEP = 4
ATTN_CHUNKS_PER_ITER = 8
ATTN_ISSUE_ORDER = "SUSUSUSU"

V7X_VMEM_LIMIT_BYTES = 48 * 1024 * 1024

NT_DIMS = (((1,), (1,)), ((), ()))


def _split3(x):
    hi = x.astype(BF16)
    r1 = x - hi.astype(F32)
    mid = r1.astype(BF16)
    lo = (r1 - mid.astype(F32)).astype(BF16)
    return hi, mid, lo


def _layer_norm(z, g, b):
    mu = jnp.mean(z, axis=-1, keepdims=True)
    zc = z - mu
    var = jnp.mean(zc * zc, axis=-1, keepdims=True)
    return zc * lax.rsqrt(var + LN_EPS) * g + b


def _const_spec(shape):
    nd = len(shape)
    return pl.BlockSpec(shape, lambda *_: (0,) * nd, pipeline_mode=pl.Buffered(1))


def _inproj_kernel(x_ref, wt_ref, wn_ref, qT_ref, k_ref, vT_ref, ff_ref, lx_ref, lg_ref, *, ts):
    si = pl.program_id(1)
    xb = x_ref[0].astype(BF16)
    t = lax.dot_general(wt_ref[...], xb, NT_DIMS, preferred_element_type=F32)
    row = lax.broadcasted_iota(jnp.int32, (AUG, ts), 0)
    ones_rows = (row >= HEAD_DIM) & (row < HEAD_DIM + N_SPLIT)
    for j in range(N_ATT):
        q = t[j * AUG:(j + 1) * AUG, :] * QK_SCALE
        if j < N_FOX:
            q = jnp.where(ones_rows, 1.0, q)
        qT_ref[0, j * AUG:(j + 1) * AUG, :] = q.astype(BF16)
    vT_ref[0] = t[N_ATT * AUG:, :].astype(BF16)

    n = jnp.dot(xb, wn_ref[...], preferred_element_type=F32)
    lane = lax.broadcasted_iota(jnp.int32, (ts, AUG), 1)
    blk = (si * ts + lax.broadcasted_iota(jnp.int32, (ts, AUG), 0)) // MOBA_BLOCK
    onehot = lane - HEAD_DIM == blk
    for j in range(N_ATT):
        k = n[:, j * AUG:(j + 1) * AUG]
        if j >= N_FOX:
            k = jnp.where(onehot, 1.0, k)
        k_ref[0, :, j * AUG:(j + 1) * AUG] = k.astype(BF16)
    base = N_ATT * AUG
    ff_ref[0] = n[:, base:base + AUG]
    lx_ref[0] = n[:, base + AUG:base + AUG + D_LRU]
    lg_ref[0] = n[:, base + AUG + D_LRU:base + AUG + 2 * D_LRU]


def _inproj(x, wt, wn, *, ts):
    B, S, D = x.shape
    grid = (B, S // ts)
    out_shape = (
        jax.ShapeDtypeStruct((B, N_ATT * AUG, S), BF16),
        jax.ShapeDtypeStruct((B, S, N_ATT * AUG), BF16),
        jax.ShapeDtypeStruct((B, D_ATT, S), BF16),
        jax.ShapeDtypeStruct((B, S, AUG), F32),
        jax.ShapeDtypeStruct((B, S, D_LRU), F32),
        jax.ShapeDtypeStruct((B, S, D_LRU), F32),
    )
    return pl.pallas_call(
        functools.partial(_inproj_kernel, ts=ts),
        grid=grid,
        in_specs=[
            pl.BlockSpec((1, ts, D), lambda b, s: (b, s, 0)),
            _const_spec(wt.shape),
            _const_spec(wn.shape),
        ],
        out_specs=(
            pl.BlockSpec((1, N_ATT * AUG, ts), lambda b, s: (b, 0, s)),
            pl.BlockSpec((1, ts, N_ATT * AUG), lambda b, s: (b, s, 0)),
            pl.BlockSpec((1, D_ATT, ts), lambda b, s: (b, 0, s)),
            pl.BlockSpec((1, ts, AUG), lambda b, s: (b, s, 0)),
            pl.BlockSpec((1, ts, D_LRU), lambda b, s: (b, s, 0)),
            pl.BlockSpec((1, ts, D_LRU), lambda b, s: (b, s, 0)),
        ),
        out_shape=out_shape,
        compiler_params=pltpu.CompilerParams(
            dimension_semantics=("arbitrary", "arbitrary"), vmem_limit_bytes=V7X_VMEM_LIMIT_BYTES),
        name="inproj",
    )(x, wt, wn)


def _foxprep_kernel(ff_ref, bf_ref, place_ref, k_ref, out_ref, tri_ref, carry_ref, *, ts):
    @pl.when(pl.program_id(1) == 0)
    def _():
        r = lax.broadcasted_iota(jnp.int32, (ts, ts), 0)
        c = lax.broadcasted_iota(jnp.int32, (ts, ts), 1)
        tri_ref[...] = (c <= r).astype(BF16)
        carry_ref[...] = jnp.zeros_like(carry_ref)

    log_f = jax.nn.log_sigmoid(ff_ref[0] + bf_ref[...])
    tri = tri_ref[...]
    cum = carry_ref[0:1, :]
    for piece in _split3(log_f):
        cum = cum + jnp.dot(tri, piece, preferred_element_type=F32)
    carry_ref[...] = jnp.broadcast_to(cum[ts - 1:ts, :], carry_ref.shape)

    pieces = jnp.concatenate(_split3(-LOG2E * cum), axis=1)
    bias = jnp.dot(pieces, place_ref[...], preferred_element_type=F32)
    lane = lax.broadcasted_iota(jnp.int32, (ts, N_FOX * AUG), 1) % AUG
    out_ref[0] = jnp.where(lane < HEAD_DIM, k_ref[0], bias.astype(BF16))


def _foxprep(ff, bf_pad, place, k_aug, *, ts):
    B, S, _ = ff.shape
    wf = N_FOX * AUG
    return pl.pallas_call(
        functools.partial(_foxprep_kernel, ts=ts),
        grid=(B, S // ts),
        in_specs=[
            pl.BlockSpec((1, ts, AUG), lambda b, s: (b, s, 0)),
            _const_spec(bf_pad.shape),
            _const_spec(place.shape),
            pl.BlockSpec((1, ts, wf), lambda b, s: (b, s, 0)),
        ],
        out_specs=pl.BlockSpec((1, ts, wf), lambda b, s: (b, s, 0)),
        out_shape=jax.ShapeDtypeStruct(k_aug.shape, k_aug.dtype),
        scratch_shapes=[pltpu.VMEM((ts, ts), BF16), pltpu.VMEM((8, AUG), F32)],
        input_output_aliases={3: 0},
        compiler_params=pltpu.CompilerParams(
            dimension_semantics=("arbitrary", "arbitrary"), vmem_limit_bytes=V7X_VMEM_LIMIT_BYTES),
        name="foxprep",
    )(ff, bf_pad, place, k_aug)


def _kmean_kernel(k_ref, out_ref, *, nblk):
    k = k_ref[0].astype(F32)
    out_ref[0] = jnp.mean(k.reshape(nblk, MOBA_BLOCK, k.shape[-1]), axis=1)


def _kmean(k_aug, *, nblk):
    B, S, _ = k_aug.shape
    nb = S // MOBA_BLOCK
    wm = N_MOBA * AUG
    return pl.pallas_call(
        functools.partial(_kmean_kernel, nblk=nblk),
        grid=(B, nb // nblk),
        in_specs=[pl.BlockSpec((1, nblk * MOBA_BLOCK, wm), lambda b, i: (b, i, 1))],
        out_specs=pl.BlockSpec((1, nblk, wm), lambda b, i: (b, i, 0)),
        out_shape=jax.ShapeDtypeStruct((B, nb, wm), F32),
        compiler_params=pltpu.CompilerParams(
            dimension_semantics=("arbitrary", "arbitrary"), vmem_limit_bytes=V7X_VMEM_LIMIT_BYTES),
        name="kmean",
    )(k_aug)


def _mobasel_kernel(kmean_ref, qT_ref, out_ref, *, nb, tq):
    cur = (pl.program_id(2) * tq + lax.broadcasted_iota(jnp.int32, (nb, tq), 1)) // MOBA_BLOCK
    lane = lax.broadcasted_iota(jnp.int32, (nb, AUG), 1)
    kmean = jnp.where(lane < HEAD_DIM, kmean_ref[0], 0.0).astype(BF16)
    qT = qT_ref[0]
    g = jnp.dot(kmean, qT, preferred_element_type=F32)
    blk = lax.broadcasted_iota(jnp.int32, (nb, tq), 0)
    past = blk < cur
    g = jnp.where(past, g, NEG)
    chosen = jnp.zeros((nb, tq), jnp.bool_)
    for _ in range(MOBA_TOPK):
        top = jnp.max(g, axis=0, keepdims=True)
        first = jnp.min(jnp.where(g == top, blk, nb), axis=0, keepdims=True)
        pick = blk == first
        chosen = chosen | pick
        g = jnp.where(pick, -jnp.inf, g)
    allowed = (chosen & past) | (blk == cur)
    out_ref[0, 0:HEAD_DIM, :] = qT[0:HEAD_DIM, :]
    bias = jnp.where(allowed, 0.0, NEG).astype(BF16)
    if nb < AUG - HEAD_DIM:
        bias = jnp.concatenate([bias, jnp.zeros((AUG - HEAD_DIM - nb, tq), BF16)], axis=0)
    out_ref[0, HEAD_DIM:AUG, :] = bias


def _mobasel(kmean, qT_aug, *, tq):
    B, nb, _ = kmean.shape
    S = qT_aug.shape[-1]
    return pl.pallas_call(
        functools.partial(_mobasel_kernel, nb=nb, tq=tq),
        grid=(B, N_MOBA, S // tq),
        in_specs=[
            pl.BlockSpec((1, nb, AUG), lambda b, h, i: (b, 0, h)),
            pl.BlockSpec((1, AUG, tq), lambda b, h, i: (b, N_FOX + h, i)),
        ],
        out_specs=pl.BlockSpec((1, AUG, tq), lambda b, h, i: (b, N_FOX + h, i)),
        out_shape=jax.ShapeDtypeStruct(qT_aug.shape, qT_aug.dtype),
        input_output_aliases={1: 0},
        compiler_params=pltpu.CompilerParams(
            dimension_semantics=("arbitrary", "arbitrary", "arbitrary"),
            vmem_limit_bytes=V7X_VMEM_LIMIT_BYTES),
        name="mobasel",
    )(kmean, qT_aug)


def _attn_kernel(qT_ref, k_ref, vT_ref, g_ref, out_ref, sa_ref, sb_ref, mxa_ref, mxb_ref, m_ref, acc_ref,
                 *, tq, kw, nh):
    qi = pl.program_id(2)
    heads = range(nh)
    ones = jnp.ones((DENOM_ROWS, kw), BF16)
    bufs = ((sa_ref, mxa_ref), (sb_ref, mxb_ref))

    def score_head(c, hh, s_ref, mx_ref):
        off = pl.multiple_of(c * kw, kw)
        s = jnp.dot(k_ref[0, pl.ds(off, kw), hh * AUG:(hh + 1) * AUG], qT_ref[0, hh * AUG:(hh + 1) * AUG, :],
                    preferred_element_type=F32)
        s_ref[hh] = s
        mx_ref[hh] = jnp.max(s, axis=0, keepdims=True)

    def update_head(c, hh, s_ref, mx_ref, masked):
        off = pl.multiple_of(c * kw, kw)
        if masked:
            kpos = off + lax.broadcasted_iota(jnp.int32, (kw, tq), 0)
            qpos = qi * tq + lax.broadcasted_iota(jnp.int32, (kw, tq), 1)
            s = jnp.where(kpos <= qpos, s_ref[hh], NEG)
            mx = jnp.max(s, axis=0, keepdims=True)
        else:
            s = s_ref[hh]
            mx = mx_ref[hh]
        m = m_ref[hh]
        m_new = jnp.maximum(m, mx)
        alpha = jnp.exp2(m - m_new)
        p = jnp.exp2(s - m_new).astype(BF16)
        vT = vT_ref[0, hh * HEAD_DIM:(hh + 1) * HEAD_DIM, pl.ds(off, kw)]
        v_ones = jnp.concatenate([vT, ones], axis=0)
        acc_ref[hh] = alpha * acc_ref[hh] + jnp.dot(v_ones, p, preferred_element_type=F32)
        m_ref[hh] = m_new

    def stage(c, parity, *, prefetch=True, masked=False):
        todo = {"S": iter(heads), "U": iter(heads)}
        for step in ATTN_ISSUE_ORDER:
            hh = next(todo[step])
            if step == "U":
                update_head(c, hh, *bufs[parity], masked)
            elif prefetch:
                score_head(c + 1, hh, *bufs[1 - parity])

    def past_chunks(first, count):
        for t in range(count):
            stage(first + t, t % 2)

    m_ref[...] = jnp.full(m_ref.shape, -jnp.inf, F32)
    acc_ref[...] = jnp.zeros(acc_ref.shape, F32)
    for hh in heads:
        score_head(0, hh, *bufs[0])

    pairs_per_iter = ATTN_CHUNKS_PER_ITER // 2

    @pl.loop(0, qi // pairs_per_iter)
    def _(j):
        past_chunks(j * ATTN_CHUNKS_PER_ITER, ATTN_CHUNKS_PER_ITER)

    for r in range(1, pairs_per_iter):
        @pl.when(qi % pairs_per_iter == r)
        def _():
            past_chunks((qi - r) * 2, 2 * r)

    stage(2 * qi, 0, masked=True)
    stage(2 * qi + 1, 1, prefetch=False, masked=True)

    normed = []
    for hh in heads:
        acc = acc_ref[hh]
        o = acc[:HEAD_DIM] * (1.0 / acc[HEAD_DIM:HEAD_DIM + 1])
        normed.append(o * lax.rsqrt(jnp.mean(o * o, axis=0, keepdims=True) + RMS_EPS))
    oT = jnp.concatenate(normed, axis=0)
    out_ref[0] = (oT.T * g_ref[0]).astype(out_ref.dtype)


def _attention(qT_aug, k_aug, vT, gain, *, tq, nh):
    B, S, _ = k_aug.shape
    kw = tq // 2
    score_buf = pltpu.VMEM((nh, kw, tq), F32)
    max_buf = pltpu.VMEM((nh, 1, tq), F32)
    acc_buf = pltpu.VMEM((nh, HEAD_DIM + DENOM_ROWS, tq), F32)
    return pl.pallas_call(
        functools.partial(_attn_kernel, tq=tq, kw=kw, nh=nh),
        grid=(B, N_ATT // nh, S // tq),
        in_specs=[
            pl.BlockSpec((1, nh * AUG, tq), lambda b, p, i: (b, p, i)),
            pl.BlockSpec((1, S, nh * AUG), lambda b, p, i: (b, 0, p), pipeline_mode=pl.Buffered(1)),
            pl.BlockSpec((1, nh * HEAD_DIM, S), lambda b, p, i: (b, p, 0), pipeline_mode=pl.Buffered(1)),
            pl.BlockSpec((1, 1, nh * HEAD_DIM), lambda b, p, i: (p, 0, 0)),
        ],
        out_specs=pl.BlockSpec((1, tq, nh * HEAD_DIM), lambda b, p, i: (b, i, p)),
        out_shape=jax.ShapeDtypeStruct((B, S, D_ATT), BF16),
        scratch_shapes=[score_buf, score_buf, max_buf, max_buf, max_buf, acc_buf],
        compiler_params=pltpu.CompilerParams(
            dimension_semantics=("arbitrary", "arbitrary", "arbitrary"),
            vmem_limit_bytes=V7X_VMEM_LIMIT_BYTES),
        name="attention",
    )(qT_aug, k_aug, vT, gain)


def _lru_kernel(lx_ref, lg_ref, cw_ref, cb_ref, wr_ref, br_ref, wi_ref, bi_ref, lam_ref, g_ref, ones_ref,
                out_ref, tail_ref, h_ref, *, tl):
    @pl.when(pl.program_id(1) == 0)
    def _():
        tail_ref[...] = jnp.zeros_like(tail_ref)
        h_ref[...] = jnp.zeros_like(h_ref)

    x = lx_ref[0]
    xx = jnp.concatenate([tail_ref[...], x], axis=0)
    tail_ref[...] = x[tl - 8:tl, :]
    xc = cb_ref[...] + x * cw_ref[CONV_WIDTH - 1:CONV_WIDTH, :]
    for d in range(1, CONV_WIDTH):
        xc = xc + xx[8 - d:8 - d + tl, :] * cw_ref[CONV_WIDTH - 1 - d:CONV_WIDTH - d, :]

    xcb = xc.astype(BF16)
    r = jax.nn.sigmoid(jnp.dot(xcb, wr_ref[...], preferred_element_type=F32) + br_ref[...])
    i = jax.nn.sigmoid(jnp.dot(xcb, wi_ref[...], preferred_element_type=F32) + bi_ref[...])
    log_a = LRU_C * r * jax.nn.log_sigmoid(lam_ref[...])
    a = jnp.exp(log_a)
    u = jnp.sqrt(-jnp.tanh(log_a) * (a * a + 1.0)) * (i * xc)

    pos = lax.broadcasted_iota(jnp.int32, (tl, D_LRU), 0)
    d = 1
    while d < tl:
        keep = pos >= d
        a_prev = jnp.where(keep, pltpu.roll(a, d, 0), 1.0)
        u_prev = jnp.where(keep, pltpu.roll(u, d, 0), 0.0)
        u = a * u_prev + u
        a = a * a_prev
        d *= 2
    h = u + a * h_ref[0:1, :]
    h_ref[...] = jnp.broadcast_to(h[tl - 1:tl, :], h_ref.shape)

    y = jax.nn.gelu(lg_ref[0]) * h
    sq_hi, sq_mid, _ = _split3(y * y)
    ms = (jnp.dot(sq_hi, ones_ref[...], preferred_element_type=F32)
          + jnp.dot(sq_mid, ones_ref[...], preferred_element_type=F32)) * (1.0 / LRU_GROUP)
    out_ref[0] = (y * lax.rsqrt(ms + RMS_EPS) * g_ref[...]).astype(out_ref.dtype)


def _lru(lx, lg, cw, cb, wr, br, wi, bi, lam, gain, ones_bd, *, tl):
    B, S, _ = lx.shape
    small = [cw, cb, wr, br, wi, bi, lam, gain, ones_bd]
    return pl.pallas_call(
        functools.partial(_lru_kernel, tl=tl),
        grid=(B, S // tl),
        in_specs=[
            pl.BlockSpec((1, tl, D_LRU), lambda b, s: (b, s, 0)),
            pl.BlockSpec((1, tl, D_LRU), lambda b, s: (b, s, 0)),
        ] + [_const_spec(a.shape) for a in small],
        out_specs=pl.BlockSpec((1, tl, D_LRU), lambda b, s: (b, s, 0)),
        out_shape=jax.ShapeDtypeStruct((B, S, D_LRU), BF16),
        scratch_shapes=[pltpu.VMEM((8, D_LRU), F32), pltpu.VMEM((8, D_LRU), F32)],
        compiler_params=pltpu.CompilerParams(
            dimension_semantics=("arbitrary", "arbitrary"), vmem_limit_bytes=V7X_VMEM_LIMIT_BYTES),
        name="rglru",
    )(lx, lg, *small)


def _outproj_kernel(x_ref, ya_ref, yl_ref, wa_ref, wl_ref, g_ref, b_ref, out_ref, *, alpha):
    mix = jnp.dot(ya_ref[0], wa_ref[...], preferred_element_type=F32)
    mix = mix + jnp.dot(yl_ref[0], wl_ref[...], preferred_element_type=F32)
    out_ref[0] = _layer_norm(alpha * x_ref[0] + mix, g_ref[...], b_ref[...])


def _outproj(x, y_att, y_lru, wa, wl, g, b, *, ts, alpha):
    B, S, D = x.shape
    return pl.pallas_call(
        functools.partial(_outproj_kernel, alpha=alpha),
        grid=(B, S // ts),
        in_specs=[
            pl.BlockSpec((1, ts, D), lambda b_, s: (b_, s, 0)),
            pl.BlockSpec((1, ts, D_ATT), lambda b_, s: (b_, s, 0)),
            pl.BlockSpec((1, ts, D_LRU), lambda b_, s: (b_, s, 0)),
            _const_spec(wa.shape), _const_spec(wl.shape), _const_spec(g.shape), _const_spec(b.shape),
        ],
        out_specs=pl.BlockSpec((1, ts, D), lambda b_, s: (b_, s, 0)),
        out_shape=jax.ShapeDtypeStruct(x.shape, F32),
        compiler_params=pltpu.CompilerParams(
            dimension_semantics=("arbitrary", "arbitrary"), vmem_limit_bytes=V7X_VMEM_LIMIT_BYTES),
        name="outproj",
    )(x, y_att, y_lru, wa, wl, g, b)


def _ffn_kernel(x_ref, wg_ref, wu_ref, wd_ref, g_ref, b_ref, out_ref, *, alpha):
    x = x_ref[0]
    xb = x.astype(BF16)
    gate = jnp.dot(xb, wg_ref[...], preferred_element_type=F32)
    up = jnp.dot(xb, wu_ref[...], preferred_element_type=F32)
    hid = (jax.nn.silu(gate) * up).astype(BF16)
    ffn = jnp.dot(hid, wd_ref[...], preferred_element_type=F32)
    out_ref[0] = _layer_norm(alpha * x + ffn, g_ref[...], b_ref[...])


def _ffn(x, wg, wu, wd, g, b, *, ts, alpha):
    B, S, D = x.shape
    return pl.pallas_call(
        functools.partial(_ffn_kernel, alpha=alpha),
        grid=(B, S // ts),
        in_specs=[
            pl.BlockSpec((1, ts, D), lambda b_, s: (b_, s, 0)),
            _const_spec(wg.shape), _const_spec(wu.shape), _const_spec(wd.shape),
            _const_spec(g.shape), _const_spec(b.shape),
        ],
        out_specs=pl.BlockSpec((1, ts, D), lambda b_, s: (b_, s, 0)),
        out_shape=jax.ShapeDtypeStruct(x.shape, F32),
        compiler_params=pltpu.CompilerParams(
            dimension_semantics=("arbitrary", "arbitrary"), vmem_limit_bytes=V7X_VMEM_LIMIT_BYTES),
        name="ffn",
    )(x, wg, wu, wd, g, b)


def _pad_heads(w):
    d = w.shape[0]
    w = w.reshape(d, -1, HEAD_DIM)
    return jnp.pad(w, ((0, 0), (0, 0), (0, AUG - HEAD_DIM))).reshape(d, -1)


def _transpose_kernel(w_ref, out_ref):
    out_ref[...] = w_ref[...].T.astype(out_ref.dtype)


def _transpose_to_bf16(w, *, tn=256):
    d, n = w.shape
    return pl.pallas_call(
        _transpose_kernel,
        grid=(n // tn,),
        in_specs=[pl.BlockSpec((d, tn), lambda j: (0, j))],
        out_specs=pl.BlockSpec((tn, d), lambda j: (j, 0)),
        out_shape=jax.ShapeDtypeStruct((n, d), BF16),
        compiler_params=pltpu.CompilerParams(
            dimension_semantics=("arbitrary",), vmem_limit_bytes=V7X_VMEM_LIMIT_BYTES),
        name="wtranspose",
    )(w)


def _layer_weights(w_in_l):
    sizes = [N_FOX * HEAD_DIM] * 3 + [N_FOX] + [N_MOBA * HEAD_DIM] * 3 + [D_LRU] * 2
    splits = [sum(sizes[:i + 1]) for i in range(len(sizes) - 1)]
    fq, fk, fv, ff, mq, mk, mv, lx, lg = jnp.split(w_in_l, splits, axis=1)
    wq = _pad_heads(jnp.concatenate([fq, mq], axis=1))
    wv = jnp.concatenate([fv, mv], axis=1)
    wt = _transpose_to_bf16(jnp.concatenate([wq, wv], axis=1))
    wk = _pad_heads(jnp.concatenate([fk, mk], axis=1))
    ffp = jnp.pad(ff, ((0, 0), (0, AUG - N_FOX)))
    wn = jnp.concatenate([wk, ffp, lx, lg], axis=1).astype(BF16)
    return wt, wn


def _block_diag(w):
    g, n, _ = w.shape
    eye = jnp.eye(g, dtype=w.dtype)
    return (eye[:, None, :, None] * w[:, :, None, :]).reshape(g * n, g * n)


def _fox_placement():
    place = jnp.zeros((N_SPLIT, AUG, N_FOX, AUG), F32)
    for c in range(N_SPLIT):
        for h in range(N_FOX):
            place = place.at[c, h, h, HEAD_DIM + c].set(1.0)
    return place.reshape(N_SPLIT * AUG, N_FOX * AUG).astype(BF16)


def _tiles(S):
    def fit(n, want):
        t = min(n, want)
        while n % t:
            t //= 2
        return t

    nb = S // MOBA_BLOCK
    return dict(proj=fit(S, 256), fox=fit(S, 512), kmean=fit(nb, 8), sel=fit(S, 2048), attn=fit(S, 512),
                lru=fit(S, 256), out=fit(S, 512), ffn=fit(S, 512))


def kernel(x, w_in, b_fgate, conv_w, conv_b, w_rgate, b_rgate, w_igate, b_igate, lru_lambda, out_norm_g, w_out,
           ln1_g, ln1_b, w_ffn_gate, w_ffn_up, w_ffn_down, ln2_g, ln2_b):
    B, S, D = x.shape
    depth = w_in.shape[0]
    assert S % MOBA_BLOCK == 0 and S // MOBA_BLOCK <= AUG - HEAD_DIM
    alpha = (2 * depth) ** 0.25
    t = _tiles(S)
    place = _fox_placement()
    ones_bd = _block_diag(jnp.ones((D_LRU // LRU_GROUP, LRU_GROUP, LRU_GROUP), BF16))
    row = lambda v: v.reshape(1, -1).astype(F32)

    for l in range(depth):
        wt, wn = _layer_weights(w_in[l])
        qT_aug, k_aug, vT, ff, lx, lg = _inproj(x, wt, wn, ts=t["proj"])

        bf_pad = jnp.pad(b_fgate[l], (0, AUG - N_FOX)).reshape(1, AUG)
        k_aug = _foxprep(ff, bf_pad, place, k_aug, ts=t["fox"])

        kmean = _kmean(k_aug, nblk=t["kmean"])
        qT_aug = _mobasel(kmean, qT_aug, tq=t["sel"])

        gain = out_norm_g[l]
        nh = ATTN_HEADS_PER_STEP
        y_att = _attention(qT_aug, k_aug, vT, gain[:D_ATT].reshape(N_ATT // nh, 1, nh * HEAD_DIM),
                           tq=t["attn"], nh=nh)

        y_lru = _lru(lx, lg, conv_w[l], row(conv_b[l]), _block_diag(w_rgate[l]).astype(BF16), row(b_rgate[l]),
                     _block_diag(w_igate[l]).astype(BF16), row(b_igate[l]), row(lru_lambda[l]),
                     row(gain[D_ATT:]), ones_bd, tl=t["lru"])

        wo = w_out[l].astype(BF16)
        x = _outproj(x, y_att, y_lru, wo[:D_ATT], wo[D_ATT:], row(ln1_g[l]), row(ln1_b[l]),
                     ts=t["out"], alpha=alpha)
        x = _ffn(x, w_ffn_gate[l].astype(BF16), w_ffn_up[l].astype(BF16), w_ffn_down[l].astype(BF16),
                 row(ln2_g[l]), row(ln2_b[l]), ts=t["ffn"], alpha=alpha)
    return x
```

```python
import functools
import math

import jax
import jax.numpy as jnp
from jax import lax
from jax.experimental import pallas as pl
from jax.experimental.pallas import tpu as pltpu

F32 = jnp.float32
BF16 = jnp.bfloat16

HEAD_DIM = 64
AUG = 128
N_FOX = 6
N_MOBA = 6
N_ATT = N_FOX + N_MOBA
D_ATT = N_ATT * HEAD_DIM
D_LRU = 256
LRU_GROUP = 64
MOBA_BLOCK = 256
MOBA_TOPK = 3
CONV_WIDTH = 4
LRU_C = 8.0
LN_EPS = 1e-5
RMS_EPS = 1e-6
NEG = -1e30
LOG2E = math.log2(math.e)
QK_SCALE = LOG2E / math.sqrt(HEAD_DIM)
DENOM_ROWS = 16
N_SPLIT = 3
assert N_FOX == N_MOBA
ATTN_HEADS_PER_STEP = 4
ATTN_CHUNKS_PER_ITER = 8
ATTN_ISSUE_ORDER = "SUSUSUSU"

V7X_VMEM_LIMIT_BYTES = 48 * 1024 * 1024

NT_DIMS = (((1,), (1,)), ((), ()))


def _split3(x):
    hi = x.astype(BF16)
    r1 = x - hi.astype(F32)
    mid = r1.astype(BF16)
    lo = (r1 - mid.astype(F32)).astype(BF16)
    return hi, mid, lo


def _layer_norm(z, g, b):
    mu = jnp.mean(z, axis=-1, keepdims=True)
    zc = z - mu
    var = jnp.mean(zc * zc, axis=-1, keepdims=True)
    return zc * lax.rsqrt(var + LN_EPS) * g + b


def _const_spec(shape):
    nd = len(shape)
    return pl.BlockSpec(shape, lambda *_: (0,) * nd, pipeline_mode=pl.Buffered(1))


def _inproj_kernel(x_ref, wt_ref, wn_ref, qT_ref, k_ref, vT_ref, ff_ref, lx_ref, lg_ref, *, ts):
    si = pl.program_id(1)
    xb = x_ref[0].astype(BF16)
    t = lax.dot_general(wt_ref[...], xb, NT_DIMS, preferred_element_type=F32)
    row = lax.broadcasted_iota(jnp.int32, (HEAD_DIM, ts), 0)
    fox_bias_rows = jnp.where(row < N_SPLIT, 1.0, 0.0).astype(BF16)
    moba_bias_rows = jnp.zeros((HEAD_DIM, ts), BF16)
    for j in range(N_ATT):
        q = (t[j * HEAD_DIM:(j + 1) * HEAD_DIM, :] * QK_SCALE).astype(BF16)
        halves = (q, fox_bias_rows) if j < N_FOX else (moba_bias_rows, q)
        qT_ref[0, j * AUG:j * AUG + HEAD_DIM, :] = halves[0]
        qT_ref[0, j * AUG + HEAD_DIM:(j + 1) * AUG, :] = halves[1]
    vT_ref[0] = t[D_ATT:, :].astype(BF16)

    n = jnp.dot(xb, wn_ref[...], preferred_element_type=F32)
    lane = lax.broadcasted_iota(jnp.int32, (ts, AUG), 1)
    blk = (si * ts + lax.broadcasted_iota(jnp.int32, (ts, AUG), 0)) // MOBA_BLOCK
    low = lane < HEAD_DIM
    onehot = jnp.where(lane == blk, 1.0, 0.0)
    for g in range(N_FOX):
        kk = n[:, g * AUG:(g + 1) * AUG]
        k_ref[0, :, g * AUG:(g + 1) * AUG] = jnp.where(low, kk, 0.0).astype(BF16)
        k_ref[0, :, (N_FOX + g) * AUG:(N_FOX + g + 1) * AUG] = jnp.where(low, onehot, kk).astype(BF16)
    base = N_FOX * AUG
    ff_ref[0] = n[:, base:base + AUG]
    lx_ref[0] = n[:, base + AUG:base + AUG + D_LRU]
    lg_ref[0] = n[:, base + AUG + D_LRU:base + AUG + 2 * D_LRU]


def _inproj(x, wt, wn, *, ts):
    B, S, D = x.shape
    grid = (B, S // ts)
    out_shape = (
        jax.ShapeDtypeStruct((B, N_ATT * AUG, S), BF16),
        jax.ShapeDtypeStruct((B, S, N_ATT * AUG), BF16),
        jax.ShapeDtypeStruct((B, D_ATT, S), BF16),
        jax.ShapeDtypeStruct((B, S, AUG), F32),
        jax.ShapeDtypeStruct((B, S, D_LRU), F32),
        jax.ShapeDtypeStruct((B, S, D_LRU), F32),
    )
    return pl.pallas_call(
        functools.partial(_inproj_kernel, ts=ts),
        grid=grid,
        in_specs=[
            pl.BlockSpec((1, ts, D), lambda b, s: (b, s, 0)),
            _const_spec(wt.shape),
            _const_spec(wn.shape),
        ],
        out_specs=(
            pl.BlockSpec((1, N_ATT * AUG, ts), lambda b, s: (b, 0, s)),
            pl.BlockSpec((1, ts, N_ATT * AUG), lambda b, s: (b, s, 0)),
            pl.BlockSpec((1, D_ATT, ts), lambda b, s: (b, 0, s)),
            pl.BlockSpec((1, ts, AUG), lambda b, s: (b, s, 0)),
            pl.BlockSpec((1, ts, D_LRU), lambda b, s: (b, s, 0)),
            pl.BlockSpec((1, ts, D_LRU), lambda b, s: (b, s, 0)),
        ),
        out_shape=out_shape,
        compiler_params=pltpu.CompilerParams(
            dimension_semantics=("arbitrary", "arbitrary"), vmem_limit_bytes=V7X_VMEM_LIMIT_BYTES),
        name="inproj",
    )(x, wt, wn)


def _foxprep_kernel(ff_ref, bf_ref, place_ref, k_ref, out_ref, tri_ref, carry_ref, *, ts):
    @pl.when(pl.program_id(1) == 0)
    def _():
        r = lax.broadcasted_iota(jnp.int32, (ts, ts), 0)
        c = lax.broadcasted_iota(jnp.int32, (ts, ts), 1)
        tri_ref[...] = (c <= r).astype(BF16)
        carry_ref[...] = jnp.zeros_like(carry_ref)

    log_f = jax.nn.log_sigmoid(ff_ref[0] + bf_ref[...])
    tri = tri_ref[...]
    cum = carry_ref[0:1, :]
    for piece in _split3(log_f):
        cum = cum + jnp.dot(tri, piece, preferred_element_type=F32)
    carry_ref[...] = jnp.broadcast_to(cum[ts - 1:ts, :], carry_ref.shape)

    pieces = jnp.concatenate(_split3(-LOG2E * cum), axis=1)
    bias = jnp.dot(pieces, place_ref[...], preferred_element_type=F32)
    lane = lax.broadcasted_iota(jnp.int32, (ts, N_FOX * AUG), 1) % AUG
    out_ref[0] = jnp.where(lane < HEAD_DIM, k_ref[0], bias.astype(BF16))


def _foxprep(ff, bf_pad, place, k_aug, *, ts):
    B, S, _ = ff.shape
    wf = N_FOX * AUG
    return pl.pallas_call(
        functools.partial(_foxprep_kernel, ts=ts),
        grid=(B, S // ts),
        in_specs=[
            pl.BlockSpec((1, ts, AUG), lambda b, s: (b, s, 0)),
            _const_spec(bf_pad.shape),
            _const_spec(place.shape),
            pl.BlockSpec((1, ts, wf), lambda b, s: (b, s, 0)),
        ],
        out_specs=pl.BlockSpec((1, ts, wf), lambda b, s: (b, s, 0)),
        out_shape=jax.ShapeDtypeStruct(k_aug.shape, k_aug.dtype),
        scratch_shapes=[pltpu.VMEM((ts, ts), BF16), pltpu.VMEM((8, AUG), F32)],
        input_output_aliases={3: 0},
        compiler_params=pltpu.CompilerParams(
            dimension_semantics=("arbitrary", "arbitrary"), vmem_limit_bytes=V7X_VMEM_LIMIT_BYTES),
        name="foxprep",
    )(ff, bf_pad, place, k_aug)


def _kmean_kernel(k_ref, out_ref, *, nblk):
    k = k_ref[0].astype(F32)
    out_ref[0] = jnp.mean(k.reshape(nblk, MOBA_BLOCK, k.shape[-1]), axis=1)


def _kmean(k_aug, *, nblk):
    B, S, _ = k_aug.shape
    nb = S // MOBA_BLOCK
    wm = N_MOBA * AUG
    return pl.pallas_call(
        functools.partial(_kmean_kernel, nblk=nblk),
        grid=(B, nb // nblk),
        in_specs=[pl.BlockSpec((1, nblk * MOBA_BLOCK, wm), lambda b, i: (b, i, 1))],
        out_specs=pl.BlockSpec((1, nblk, wm), lambda b, i: (b, i, 0)),
        out_shape=jax.ShapeDtypeStruct((B, nb, wm), F32),
        compiler_params=pltpu.CompilerParams(
            dimension_semantics=("arbitrary", "arbitrary"), vmem_limit_bytes=V7X_VMEM_LIMIT_BYTES),
        name="kmean",
    )(k_aug)


def _mobasel_kernel(kmean_ref, qT_ref, out_ref, *, nb, tq):
    cur = (pl.program_id(2) * tq + lax.broadcasted_iota(jnp.int32, (nb, tq), 1)) // MOBA_BLOCK
    lane = lax.broadcasted_iota(jnp.int32, (nb, AUG), 1)
    kmean = jnp.where(lane >= HEAD_DIM, kmean_ref[0], 0.0).astype(BF16)
    qT = qT_ref[0]
    g = jnp.dot(kmean, qT, preferred_element_type=F32)
    blk = lax.broadcasted_iota(jnp.int32, (nb, tq), 0)
    past = blk < cur
    g = jnp.where(past, g, NEG)
    chosen = jnp.zeros((nb, tq), jnp.bool_)
    for _ in range(MOBA_TOPK):
        top = jnp.max(g, axis=0, keepdims=True)
        first = jnp.min(jnp.where(g == top, blk, nb), axis=0, keepdims=True)
        pick = blk == first
        chosen = chosen | pick
        g = jnp.where(pick, -jnp.inf, g)
    allowed = (chosen & past) | (blk == cur)
    bias = jnp.where(allowed, 0.0, NEG).astype(BF16)
    if nb < HEAD_DIM:
        bias = jnp.concatenate([bias, jnp.zeros((HEAD_DIM - nb, tq), BF16)], axis=0)
    out_ref[0, 0:HEAD_DIM, :] = bias
    out_ref[0, HEAD_DIM:AUG, :] = qT[HEAD_DIM:AUG, :]


def _mobasel(kmean, qT_aug, *, tq):
    B, nb, _ = kmean.shape
    S = qT_aug.shape[-1]
    return pl.pallas_call(
        functools.partial(_mobasel_kernel, nb=nb, tq=tq),
        grid=(B, N_MOBA, S // tq),
        in_specs=[
            pl.BlockSpec((1, nb, AUG), lambda b, h, i: (b, 0, h)),
            pl.BlockSpec((1, AUG, tq), lambda b, h, i: (b, N_FOX + h, i)),
        ],
        out_specs=pl.BlockSpec((1, AUG, tq), lambda b, h, i: (b, N_FOX + h, i)),
        out_shape=jax.ShapeDtypeStruct(qT_aug.shape, qT_aug.dtype),
        input_output_aliases={1: 0},
        compiler_params=pltpu.CompilerParams(
            dimension_semantics=("arbitrary", "arbitrary", "arbitrary"),
            vmem_limit_bytes=V7X_VMEM_LIMIT_BYTES),
        name="mobasel",
    )(kmean, qT_aug)


def _attn_kernel(qT_ref, k_ref, vT_ref, g_ref, out_ref, sa_ref, sb_ref, mxa_ref, mxb_ref, m_ref, acc_ref,
                 *, tq, kw, nh):
    qi = pl.program_id(2)
    heads = range(nh)
    ones = jnp.ones((DENOM_ROWS, kw), BF16)
    bufs = ((sa_ref, mxa_ref), (sb_ref, mxb_ref))

    def score_head(c, hh, s_ref, mx_ref):
        off = pl.multiple_of(c * kw, kw)
        s = jnp.dot(k_ref[0, pl.ds(off, kw), hh * AUG:(hh + 1) * AUG], qT_ref[0, hh * AUG:(hh + 1) * AUG, :],
                    preferred_element_type=F32)
        s_ref[hh] = s
        mx_ref[hh] = jnp.max(s, axis=0, keepdims=True)

    def update_head(c, hh, s_ref, mx_ref, masked):
        off = pl.multiple_of(c * kw, kw)
        if masked:
            kpos = off + lax.broadcasted_iota(jnp.int32, (kw, tq), 0)
            qpos = qi * tq + lax.broadcasted_iota(jnp.int32, (kw, tq), 1)
            s = jnp.where(kpos <= qpos, s_ref[hh], NEG)
            mx = jnp.max(s, axis=0, keepdims=True)
        else:
            s = s_ref[hh]
            mx = mx_ref[hh]
        m = m_ref[hh]
        m_new = jnp.maximum(m, mx)
        alpha = jnp.exp2(m - m_new)
        p = jnp.exp2(s - m_new).astype(BF16)
        vT = vT_ref[0, hh * HEAD_DIM:(hh + 1) * HEAD_DIM, pl.ds(off, kw)]
        v_ones = jnp.concatenate([vT, ones], axis=0)
        acc_ref[hh] = alpha * acc_ref[hh] + jnp.dot(v_ones, p, preferred_element_type=F32)
        m_ref[hh] = m_new

    def stage(c, parity, *, prefetch=True, masked=False):
        todo = {"S": iter(heads), "U": iter(heads)}
        for step in ATTN_ISSUE_ORDER:
            hh = next(todo[step])
            if step == "U":
                update_head(c, hh, *bufs[parity], masked)
            elif prefetch:
                score_head(c + 1, hh, *bufs[1 - parity])

    def past_chunks(first, count):
        for t in range(count):
            stage(first + t, t % 2)

    m_ref[...] = jnp.full(m_ref.shape, -jnp.inf, F32)
    acc_ref[...] = jnp.zeros(acc_ref.shape, F32)
    for hh in heads:
        score_head(0, hh, *bufs[0])

    pairs_per_iter = ATTN_CHUNKS_PER_ITER // 2

    @pl.loop(0, qi // pairs_per_iter)
    def _(j):
        past_chunks(j * ATTN_CHUNKS_PER_ITER, ATTN_CHUNKS_PER_ITER)

    for r in range(1, pairs_per_iter):
        @pl.when(qi % pairs_per_iter == r)
        def _():
            past_chunks((qi - r) * 2, 2 * r)

    stage(2 * qi, 0, masked=True)
    stage(2 * qi + 1, 1, prefetch=False, masked=True)

    normed = []
    for hh in heads:
        acc = acc_ref[hh]
        o = acc[:HEAD_DIM] * (1.0 / acc[HEAD_DIM:HEAD_DIM + 1])
        normed.append(o * lax.rsqrt(jnp.mean(o * o, axis=0, keepdims=True) + RMS_EPS))
    oT = jnp.concatenate(normed, axis=0)
    out_ref[0] = (oT.T * g_ref[0]).astype(out_ref.dtype)


def _attention(qT_aug, k_aug, vT, gain, *, tq, nh):
    B, S, _ = k_aug.shape
    kw = tq // 2
    score_buf = pltpu.VMEM((nh, kw, tq), F32)
    max_buf = pltpu.VMEM((nh, 1, tq), F32)
    acc_buf = pltpu.VMEM((nh, HEAD_DIM + DENOM_ROWS, tq), F32)
    return pl.pallas_call(
        functools.partial(_attn_kernel, tq=tq, kw=kw, nh=nh),
        grid=(B, N_ATT // nh, S // tq),
        in_specs=[
            pl.BlockSpec((1, nh * AUG, tq), lambda b, p, i: (b, p, i)),
            pl.BlockSpec((1, S, nh * AUG), lambda b, p, i: (b, 0, p), pipeline_mode=pl.Buffered(1)),
            pl.BlockSpec((1, nh * HEAD_DIM, S), lambda b, p, i: (b, p, 0), pipeline_mode=pl.Buffered(1)),
            pl.BlockSpec((1, 1, nh * HEAD_DIM), lambda b, p, i: (p, 0, 0)),
        ],
        out_specs=pl.BlockSpec((1, tq, nh * HEAD_DIM), lambda b, p, i: (b, i, p)),
        out_shape=jax.ShapeDtypeStruct((B, S, D_ATT), BF16),
        scratch_shapes=[score_buf, score_buf, max_buf, max_buf, max_buf, acc_buf],
        compiler_params=pltpu.CompilerParams(
            dimension_semantics=("arbitrary", "arbitrary", "arbitrary"),
            vmem_limit_bytes=V7X_VMEM_LIMIT_BYTES),
        name="attention",
    )(qT_aug, k_aug, vT, gain)


def _lru_kernel(lx_ref, lg_ref, cw_ref, cb_ref, wr_ref, br_ref, wi_ref, bi_ref, lam_ref, g_ref, ones_ref,
                out_ref, tail_ref, h_ref, *, tl):
    @pl.when(pl.program_id(1) == 0)
    def _():
        tail_ref[...] = jnp.zeros_like(tail_ref)
        h_ref[...] = jnp.zeros_like(h_ref)

    x = lx_ref[0]
    xx = jnp.concatenate([tail_ref[...], x], axis=0)
    tail_ref[...] = x[tl - 8:tl, :]
    xc = cb_ref[...] + x * cw_ref[CONV_WIDTH - 1:CONV_WIDTH, :]
    for d in range(1, CONV_WIDTH):
        xc = xc + xx[8 - d:8 - d + tl, :] * cw_ref[CONV_WIDTH - 1 - d:CONV_WIDTH - d, :]

    xcb = xc.astype(BF16)
    r = jax.nn.sigmoid(jnp.dot(xcb, wr_ref[...], preferred_element_type=F32) + br_ref[...])
    i = jax.nn.sigmoid(jnp.dot(xcb, wi_ref[...], preferred_element_type=F32) + bi_ref[...])
    log_a = LRU_C * r * jax.nn.log_sigmoid(lam_ref[...])
    a = jnp.exp(log_a)
    u = jnp.sqrt(-jnp.tanh(log_a) * (a * a + 1.0)) * (i * xc)

    pos = lax.broadcasted_iota(jnp.int32, (tl, D_LRU), 0)
    d = 1
    while d < tl:
        keep = pos >= d
        a_prev = jnp.where(keep, pltpu.roll(a, d, 0), 1.0)
        u_prev = jnp.where(keep, pltpu.roll(u, d, 0), 0.0)
        u = a * u_prev + u
        a = a * a_prev
        d *= 2
    h = u + a * h_ref[0:1, :]
    h_ref[...] = jnp.broadcast_to(h[tl - 1:tl, :], h_ref.shape)

    y = jax.nn.gelu(lg_ref[0]) * h
    sq_hi, sq_mid, _ = _split3(y * y)
    ms = (jnp.dot(sq_hi, ones_ref[...], preferred_element_type=F32)
          + jnp.dot(sq_mid, ones_ref[...], preferred_element_type=F32)) * (1.0 / LRU_GROUP)
    out_ref[0] = (y * lax.rsqrt(ms + RMS_EPS) * g_ref[...]).astype(out_ref.dtype)


def _lru(lx, lg, cw, cb, wr, br, wi, bi, lam, gain, ones_bd, *, tl):
    B, S, _ = lx.shape
    small = [cw, cb, wr, br, wi, bi, lam, gain, ones_bd]
    return pl.pallas_call(
        functools.partial(_lru_kernel, tl=tl),
        grid=(B, S // tl),
        in_specs=[
            pl.BlockSpec((1, tl, D_LRU), lambda b, s: (b, s, 0)),
            pl.BlockSpec((1, tl, D_LRU), lambda b, s: (b, s, 0)),
        ] + [_const_spec(a.shape) for a in small],
        out_specs=pl.BlockSpec((1, tl, D_LRU), lambda b, s: (b, s, 0)),
        out_shape=jax.ShapeDtypeStruct((B, S, D_LRU), BF16),
        scratch_shapes=[pltpu.VMEM((8, D_LRU), F32), pltpu.VMEM((8, D_LRU), F32)],
        compiler_params=pltpu.CompilerParams(
            dimension_semantics=("arbitrary", "arbitrary"), vmem_limit_bytes=V7X_VMEM_LIMIT_BYTES),
        name="rglru",
    )(lx, lg, *small)


def _tail_kernel(x_ref, ya_ref, yl_ref, wa_ref, wl_ref, g1_ref, b1_ref, wg_ref, wu_ref, wd_ref, g2_ref, b2_ref,
                 out_ref, *, alpha):
    mix = jnp.dot(ya_ref[0], wa_ref[...], preferred_element_type=F32)
    mix = mix + jnp.dot(yl_ref[0], wl_ref[...], preferred_element_type=F32)
    x1 = _layer_norm(alpha * x_ref[0] + mix, g1_ref[...], b1_ref[...])

    xb = x1.astype(BF16)
    gate = jnp.dot(xb, wg_ref[...], preferred_element_type=F32)
    up = jnp.dot(xb, wu_ref[...], preferred_element_type=F32)
    hid = (jax.nn.silu(gate) * up).astype(BF16)
    ffn = jnp.dot(hid, wd_ref[...], preferred_element_type=F32)
    out_ref[0] = _layer_norm(alpha * x1 + ffn, g2_ref[...], b2_ref[...])


def _layer_tail(x, y_att, y_lru, wa, wl, g1, b1, wg, wu, wd, g2, b2, *, ts, alpha):
    B, S, D = x.shape
    consts = [wa, wl, g1, b1, wg, wu, wd, g2, b2]
    return pl.pallas_call(
        functools.partial(_tail_kernel, alpha=alpha),
        grid=(B, S // ts),
        in_specs=[
            pl.BlockSpec((1, ts, D), lambda b_, s: (b_, s, 0)),
            pl.BlockSpec((1, ts, D_ATT), lambda b_, s: (b_, s, 0)),
            pl.BlockSpec((1, ts, D_LRU), lambda b_, s: (b_, s, 0)),
        ] + [_const_spec(c.shape) for c in consts],
        out_specs=pl.BlockSpec((1, ts, D), lambda b_, s: (b_, s, 0)),
        out_shape=jax.ShapeDtypeStruct(x.shape, F32),
        compiler_params=pltpu.CompilerParams(
            dimension_semantics=("arbitrary", "arbitrary"), vmem_limit_bytes=V7X_VMEM_LIMIT_BYTES),
        name="layertail",
    )(x, y_att, y_lru, *consts)


def _pair_heads(fox, moba):
    d = fox.shape[0]
    both = jnp.stack([fox.reshape(d, N_FOX, HEAD_DIM), moba.reshape(d, N_MOBA, HEAD_DIM)], axis=2)
    return both.reshape(d, N_FOX * AUG)


def _transpose_kernel(w_ref, out_ref):
    out_ref[...] = w_ref[...].T.astype(out_ref.dtype)


def _transpose_to_bf16(w, *, tn=256):
    d, n = w.shape
    return pl.pallas_call(
        _transpose_kernel,
        grid=(n // tn,),
        in_specs=[pl.BlockSpec((d, tn), lambda j: (0, j))],
        out_specs=pl.BlockSpec((tn, d), lambda j: (j, 0)),
        out_shape=jax.ShapeDtypeStruct((n, d), BF16),
        compiler_params=pltpu.CompilerParams(
            dimension_semantics=("arbitrary",), vmem_limit_bytes=V7X_VMEM_LIMIT_BYTES),
        name="wtranspose",
    )(w)


def _layer_weights(w_in_l):
    sizes = [N_FOX * HEAD_DIM] * 3 + [N_FOX] + [N_MOBA * HEAD_DIM] * 3 + [D_LRU] * 2
    splits = [sum(sizes[:i + 1]) for i in range(len(sizes) - 1)]
    fq, fk, fv, ff, mq, mk, mv, lx, lg = jnp.split(w_in_l, splits, axis=1)
    wt = _transpose_to_bf16(jnp.concatenate([fq, mq, fv, mv], axis=1))
    ffp = jnp.pad(ff, ((0, 0), (0, AUG - N_FOX)))
    wn = jnp.concatenate([_pair_heads(fk, mk), ffp, lx, lg], axis=1).astype(BF16)
    return wt, wn


def _block_diag(w):
    g, n, _ = w.shape
    eye = jnp.eye(g, dtype=w.dtype)
    return (eye[:, None, :, None] * w[:, :, None, :]).reshape(g * n, g * n)


def _fox_placement():
    place = jnp.zeros((N_SPLIT, AUG, N_FOX, AUG), F32)
    for c in range(N_SPLIT):
        for h in range(N_FOX):
            place = place.at[c, h, h, HEAD_DIM + c].set(1.0)
    return place.reshape(N_SPLIT * AUG, N_FOX * AUG).astype(BF16)


def _tiles(S):
    def fit(n, want):
        t = min(n, want)
        while n % t:
            t //= 2
        return t

    nb = S // MOBA_BLOCK
    return dict(proj=fit(S, 256), fox=fit(S, 512), kmean=fit(nb, 8), sel=fit(S, 2048), attn=fit(S, 512),
                lru=fit(S, 256), tail=fit(S, 512))


def kernel(x, w_in, b_fgate, conv_w, conv_b, w_rgate, b_rgate, w_igate, b_igate, lru_lambda, out_norm_g, w_out,
           ln1_g, ln1_b, w_ffn_gate, w_ffn_up, w_ffn_down, ln2_g, ln2_b):
    B, S, D = x.shape
    depth = w_in.shape[0]
    assert S % MOBA_BLOCK == 0 and S // MOBA_BLOCK <= AUG - HEAD_DIM
    alpha = (2 * depth) ** 0.25
    t = _tiles(S)
    place = _fox_placement()
    ones_bd = _block_diag(jnp.ones((D_LRU // LRU_GROUP, LRU_GROUP, LRU_GROUP), BF16))
    row = lambda v: v.reshape(1, -1).astype(F32)

    for l in range(depth):
        wt, wn = _layer_weights(w_in[l])
        qT_aug, k_aug, vT, ff, lx, lg = _inproj(x, wt, wn, ts=t["proj"])

        bf_pad = jnp.pad(b_fgate[l], (0, AUG - N_FOX)).reshape(1, AUG)
        k_aug = _foxprep(ff, bf_pad, place, k_aug, ts=t["fox"])

        kmean = _kmean(k_aug, nblk=t["kmean"])
        qT_aug = _mobasel(kmean, qT_aug, tq=t["sel"])

        gain = out_norm_g[l]
        nh = ATTN_HEADS_PER_STEP
        y_att = _attention(qT_aug, k_aug, vT, gain[:D_ATT].reshape(N_ATT // nh, 1, nh * HEAD_DIM),
                           tq=t["attn"], nh=nh)

        y_lru = _lru(lx, lg, conv_w[l], row(conv_b[l]), _block_diag(w_rgate[l]).astype(BF16), row(b_rgate[l]),
                     _block_diag(w_igate[l]).astype(BF16), row(b_igate[l]), row(lru_lambda[l]),
                     row(gain[D_ATT:]), ones_bd, tl=t["lru"])

        wo = w_out[l].astype(BF16)
        x = _layer_tail(x, y_att, y_lru, wo[:D_ATT], wo[D_ATT:], row(ln1_g[l]), row(ln1_b[l]),
                        w_ffn_gate[l].astype(BF16), w_ffn_up[l].astype(BF16), w_ffn_down[l].astype(BF16),
                        row(ln2_g[l]), row(ln2_b[l]), ts=t["tail"], alpha=alpha)
    return x
```

```python
import functools
import math

import jax
import jax.numpy as jnp
from jax import lax
from jax.experimental import pallas as pl
from jax.experimental.pallas import tpu as pltpu

F32 = jnp.float32
BF16 = jnp.bfloat16

HEAD_DIM = 64
AUG = 128
N_FOX = 6
N_MOBA = 6
N_ATT = N_FOX + N_MOBA
D_ATT = N_ATT * HEAD_DIM
D_LRU = 256
LRU_GROUP = 64
MOBA_BLOCK = 256
MOBA_TOPK = 3
CONV_WIDTH = 4
LRU_C = 8.0
LN_EPS = 1e-5
RMS_EPS = 1e-6
NEG = -1e30
LOG2E = math.log2(math.e)
QK_SCALE = LOG2E / math.sqrt(HEAD_DIM)
DENOM_ROWS = 16
N_SPLIT = 3
assert N_FOX == N_MOBA
ATTN_HEADS_PER_STEP = 4
ATTN_CHUNKS_PER_ITER = 8
ATTN_ISSUE_ORDER = "SUSUSUSU"

V7X_VMEM_LIMIT_BYTES = 48 * 1024 * 1024

NT_DIMS = (((1,), (1,)), ((), ()))


def _split3(x):
    hi = x.astype(BF16)
    r1 = x - hi.astype(F32)
    mid = r1.astype(BF16)
    lo = (r1 - mid.astype(F32)).astype(BF16)
    return hi, mid, lo


def _layer_norm(z, g, b):
    mu = jnp.mean(z, axis=-1, keepdims=True)
    zc = z - mu
    var = jnp.mean(zc * zc, axis=-1, keepdims=True)
    return zc * lax.rsqrt(var + LN_EPS) * g + b


def _const_spec(shape):
    nd = len(shape)
    return pl.BlockSpec(shape, lambda *_: (0,) * nd, pipeline_mode=pl.Buffered(1))


def _moba_block_bias(gate, cur):
    nblk, tq = gate.shape
    blk = lax.broadcasted_iota(jnp.int32, (nblk, tq), 0)
    past = blk < cur
    g = jnp.where(past, gate, NEG)
    chosen = jnp.zeros((nblk, tq), jnp.bool_)
    for _ in range(MOBA_TOPK):
        top = jnp.max(g, axis=0, keepdims=True)
        first = jnp.min(jnp.where(g == top, blk, nblk), axis=0, keepdims=True)
        pick = blk == first
        chosen = chosen | pick
        g = jnp.where(pick, -jnp.inf, g)
    allowed = (chosen & past) | (blk == cur)
    return jnp.where(allowed, 0.0, NEG)


def _inproj_kernel(x_ref, wt_ref, wn_ref, bf_ref, place_ref, qT_ref, k_ref, vT_ref, lx_ref, lg_ref,
                   tri_ref, fsum_ref, kmean_ref):
    ts = MOBA_BLOCK
    si = pl.program_id(1)

    @pl.when(si == 0)
    def _():
        r = lax.broadcasted_iota(jnp.int32, (ts, ts), 0)
        c = lax.broadcasted_iota(jnp.int32, (ts, ts), 1)
        tri_ref[...] = (c <= r).astype(BF16)
        fsum_ref[...] = jnp.zeros_like(fsum_ref)
        kmean_ref[...] = jnp.zeros_like(kmean_ref)

    xb = x_ref[0].astype(BF16)
    t = lax.dot_general(wt_ref[...], xb, NT_DIMS, preferred_element_type=F32)
    n = jnp.dot(xb, wn_ref[...], preferred_element_type=F32)
    base = N_FOX * AUG
    vT_ref[0] = t[D_ATT:, :].astype(BF16)
    lx_ref[0] = n[:, base + AUG:base + AUG + D_LRU]
    lg_ref[0] = n[:, base + AUG + D_LRU:base + AUG + 2 * D_LRU]

    log_f = jax.nn.log_sigmoid(n[:, base:base + AUG] + bf_ref[...])
    cum = fsum_ref[0:1, :]
    for piece in _split3(log_f):
        cum = cum + jnp.dot(tri_ref[...], piece, preferred_element_type=F32)
    fsum_ref[...] = jnp.broadcast_to(cum[ts - 1:ts, :], fsum_ref.shape)
    pieces = jnp.concatenate(_split3(-LOG2E * cum), axis=1)
    fox_bias = jnp.dot(pieces, place_ref[...], preferred_element_type=F32)

    lane = lax.broadcasted_iota(jnp.int32, (ts, AUG), 1)
    low = lane < HEAD_DIM
    onehot = jnp.where(lane == si, 1.0, 0.0)
    for g in range(N_FOX):
        grp = slice(g * AUG, (g + 1) * AUG)
        kk = n[:, grp]
        k_ref[0, :, grp] = jnp.where(low, kk, fox_bias[:, grp]).astype(BF16)
        k_ref[0, :, (N_FOX + g) * AUG:(N_FOX + g + 1) * AUG] = jnp.where(low, onehot, kk).astype(BF16)

    kmean = kmean_ref[...]
    row = lax.broadcasted_iota(jnp.int32, (HEAD_DIM, ts), 0)
    fox_bias_rows = jnp.where(row < N_SPLIT, 1.0, 0.0).astype(BF16)
    for j in range(N_ATT):
        q = (t[j * HEAD_DIM:(j + 1) * HEAD_DIM, :] * QK_SCALE).astype(BF16)
        if j < N_FOX:
            halves = (q, fox_bias_rows)
        else:
            km = kmean[:, (j - N_FOX) * AUG + HEAD_DIM:(j - N_FOX + 1) * AUG].astype(BF16)
            gate = jnp.dot(km, q, preferred_element_type=F32)
            halves = (_moba_block_bias(gate, si).astype(BF16), q)
        qT_ref[0, j * AUG:j * AUG + HEAD_DIM, :] = halves[0]
        qT_ref[0, j * AUG + HEAD_DIM:(j + 1) * AUG, :] = halves[1]

    block_mean = jnp.mean(n[:, 0:base], axis=0, keepdims=True)
    blk_row = lax.broadcasted_iota(jnp.int32, kmean.shape, 0)
    kmean_ref[...] = jnp.where(blk_row == si, block_mean, kmean)


def _inproj(x, wt, wn, bf_pad, place):
    B, S, D = x.shape
    ts = MOBA_BLOCK
    out_shape = (
        jax.ShapeDtypeStruct((B, N_ATT * AUG, S), BF16),
        jax.ShapeDtypeStruct((B, S, N_ATT * AUG), BF16),
        jax.ShapeDtypeStruct((B, D_ATT, S), BF16),
        jax.ShapeDtypeStruct((B, S, D_LRU), F32),
        jax.ShapeDtypeStruct((B, S, D_LRU), F32),
    )
    return pl.pallas_call(
        _inproj_kernel,
        grid=(B, S // ts),
        in_specs=[
            pl.BlockSpec((1, ts, D), lambda b, s: (b, s, 0)),
            _const_spec(wt.shape), _const_spec(wn.shape), _const_spec(bf_pad.shape), _const_spec(place.shape),
        ],
        out_specs=(
            pl.BlockSpec((1, N_ATT * AUG, ts), lambda b, s: (b, 0, s)),
            pl.BlockSpec((1, ts, N_ATT * AUG), lambda b, s: (b, s, 0)),
            pl.BlockSpec((1, D_ATT, ts), lambda b, s: (b, 0, s)),
            pl.BlockSpec((1, ts, D_LRU), lambda b, s: (b, s, 0)),
            pl.BlockSpec((1, ts, D_LRU), lambda b, s: (b, s, 0)),
        ),
        out_shape=out_shape,
        scratch_shapes=[
            pltpu.VMEM((ts, ts), BF16),
            pltpu.VMEM((8, AUG), F32),
            pltpu.VMEM((HEAD_DIM, N_FOX * AUG), F32),
        ],
        compiler_params=pltpu.CompilerParams(
            dimension_semantics=("arbitrary", "arbitrary"), vmem_limit_bytes=V7X_VMEM_LIMIT_BYTES),
        name="inproj",
    )(x, wt, wn, bf_pad, place)


def _attn_kernel(qT_ref, k_ref, vT_ref, g_ref, out_ref, sa_ref, sb_ref, mxa_ref, mxb_ref, m_ref, acc_ref,
                 *, tq, kw, nh):
    qi = pl.program_id(2)
    heads = range(nh)
    ones = jnp.ones((DENOM_ROWS, kw), BF16)
    bufs = ((sa_ref, mxa_ref), (sb_ref, mxb_ref))

    def score_head(c, hh, s_ref, mx_ref):
        off = pl.multiple_of(c * kw, kw)
        s = jnp.dot(k_ref[0, pl.ds(off, kw), hh * AUG:(hh + 1) * AUG], qT_ref[0, hh * AUG:(hh + 1) * AUG, :],
                    preferred_element_type=F32)
        s_ref[hh] = s
        mx_ref[hh] = jnp.max(s, axis=0, keepdims=True)

    def update_head(c, hh, s_ref, mx_ref, masked):
        off = pl.multiple_of(c * kw, kw)
        if masked:
            kpos = off + lax.broadcasted_iota(jnp.int32, (kw, tq), 0)
            qpos = qi * tq + lax.broadcasted_iota(jnp.int32, (kw, tq), 1)
            s = jnp.where(kpos <= qpos, s_ref[hh], NEG)
            mx = jnp.max(s, axis=0, keepdims=True)
        else:
            s = s_ref[hh]
            mx = mx_ref[hh]
        m = m_ref[hh]
        m_new = jnp.maximum(m, mx)
        alpha = jnp.exp2(m - m_new)
        p = jnp.exp2(s - m_new).astype(BF16)
        vT = vT_ref[0, hh * HEAD_DIM:(hh + 1) * HEAD_DIM, pl.ds(off, kw)]
        v_ones = jnp.concatenate([vT, ones], axis=0)
        acc_ref[hh] = alpha * acc_ref[hh] + jnp.dot(v_ones, p, preferred_element_type=F32)
        m_ref[hh] = m_new

    def stage(c, parity, *, prefetch=True, masked=False):
        todo = {"S": iter(heads), "U": iter(heads)}
        for step in ATTN_ISSUE_ORDER:
            hh = next(todo[step])
            if step == "U":
                update_head(c, hh, *bufs[parity], masked)
            elif prefetch:
                score_head(c + 1, hh, *bufs[1 - parity])

    def past_chunks(first, count):
        for t in range(count):
            stage(first + t, t % 2)

    m_ref[...] = jnp.full(m_ref.shape, -jnp.inf, F32)
    acc_ref[...] = jnp.zeros(acc_ref.shape, F32)
    for hh in heads:
        score_head(0, hh, *bufs[0])

    pairs_per_iter = ATTN_CHUNKS_PER_ITER // 2

    @pl.loop(0, qi // pairs_per_iter)
    def _(j):
        past_chunks(j * ATTN_CHUNKS_PER_ITER, ATTN_CHUNKS_PER_ITER)

    for r in range(1, pairs_per_iter):
        @pl.when(qi % pairs_per_iter == r)
        def _():
            past_chunks((qi - r) * 2, 2 * r)

    stage(2 * qi, 0, masked=True)
    stage(2 * qi + 1, 1, prefetch=False, masked=True)

    normed = []
    for hh in heads:
        acc = acc_ref[hh]
        o = acc[:HEAD_DIM] * (1.0 / acc[HEAD_DIM:HEAD_DIM + 1])
        normed.append(o * lax.rsqrt(jnp.mean(o * o, axis=0, keepdims=True) + RMS_EPS))
    oT = jnp.concatenate(normed, axis=0)
    out_ref[0] = (oT.T * g_ref[0]).astype(out_ref.dtype)


def _attention(qT_aug, k_aug, vT, gain, *, tq, nh):
    B, S, _ = k_aug.shape
    kw = tq // 2
    score_buf = pltpu.VMEM((nh, kw, tq), F32)
    max_buf = pltpu.VMEM((nh, 1, tq), F32)
    acc_buf = pltpu.VMEM((nh, HEAD_DIM + DENOM_ROWS, tq), F32)
    return pl.pallas_call(
        functools.partial(_attn_kernel, tq=tq, kw=kw, nh=nh),
        grid=(B, N_ATT // nh, S // tq),
        in_specs=[
            pl.BlockSpec((1, nh * AUG, tq), lambda b, p, i: (b, p, i)),
            pl.BlockSpec((1, S, nh * AUG), lambda b, p, i: (b, 0, p), pipeline_mode=pl.Buffered(1)),
            pl.BlockSpec((1, nh * HEAD_DIM, S), lambda b, p, i: (b, p, 0), pipeline_mode=pl.Buffered(1)),
            pl.BlockSpec((1, 1, nh * HEAD_DIM), lambda b, p, i: (p, 0, 0)),
        ],
        out_specs=pl.BlockSpec((1, tq, nh * HEAD_DIM), lambda b, p, i: (b, i, p)),
        out_shape=jax.ShapeDtypeStruct((B, S, D_ATT), BF16),
        scratch_shapes=[score_buf, score_buf, max_buf, max_buf, max_buf, acc_buf],
        compiler_params=pltpu.CompilerParams(
            dimension_semantics=("arbitrary", "arbitrary", "arbitrary"),
            vmem_limit_bytes=V7X_VMEM_LIMIT_BYTES),
        name="attention",
    )(qT_aug, k_aug, vT, gain)


def _lru_kernel(lx_ref, lg_ref, cw_ref, cb_ref, wr_ref, br_ref, wi_ref, bi_ref, lam_ref, g_ref, ones_ref,
                out_ref, tail_ref, h_ref, *, tl):
    @pl.when(pl.program_id(1) == 0)
    def _():
        tail_ref[...] = jnp.zeros_like(tail_ref)
        h_ref[...] = jnp.zeros_like(h_ref)

    x = lx_ref[0]
    xx = jnp.concatenate([tail_ref[...], x], axis=0)
    tail_ref[...] = x[tl - 8:tl, :]
    xc = cb_ref[...] + x * cw_ref[CONV_WIDTH - 1:CONV_WIDTH, :]
    for d in range(1, CONV_WIDTH):
        xc = xc + xx[8 - d:8 - d + tl, :] * cw_ref[CONV_WIDTH - 1 - d:CONV_WIDTH - d, :]

    xcb = xc.astype(BF16)
    r = jax.nn.sigmoid(jnp.dot(xcb, wr_ref[...], preferred_element_type=F32) + br_ref[...])
    i = jax.nn.sigmoid(jnp.dot(xcb, wi_ref[...], preferred_element_type=F32) + bi_ref[...])
    log_a = LRU_C * r * jax.nn.log_sigmoid(lam_ref[...])
    a = jnp.exp(log_a)
    u = jnp.sqrt(-jnp.tanh(log_a) * (a * a + 1.0)) * (i * xc)

    pos = lax.broadcasted_iota(jnp.int32, (tl, D_LRU), 0)
    d = 1
    while d < tl:
        keep = pos >= d
        a_prev = jnp.where(keep, pltpu.roll(a, d, 0), 1.0)
        u_prev = jnp.where(keep, pltpu.roll(u, d, 0), 0.0)
        u = a * u_prev + u
        a = a * a_prev
        d *= 2
    h = u + a * h_ref[0:1, :]
    h_ref[...] = jnp.broadcast_to(h[tl - 1:tl, :], h_ref.shape)

    y = jax.nn.gelu(lg_ref[0]) * h
    sq_hi, sq_mid, _ = _split3(y * y)
    ms = (jnp.dot(sq_hi, ones_ref[...], preferred_element_type=F32)
          + jnp.dot(sq_mid, ones_ref[...], preferred_element_type=F32)) * (1.0 / LRU_GROUP)
    out_ref[0] = (y * lax.rsqrt(ms + RMS_EPS) * g_ref[...]).astype(out_ref.dtype)


def _lru(lx, lg, cw, cb, wr, br, wi, bi, lam, gain, ones_bd, *, tl):
    B, S, _ = lx.shape
    small = [cw, cb, wr, br, wi, bi, lam, gain, ones_bd]
    return pl.pallas_call(
        functools.partial(_lru_kernel, tl=tl),
        grid=(B, S // tl),
        in_specs=[
            pl.BlockSpec((1, tl, D_LRU), lambda b, s: (b, s, 0)),
            pl.BlockSpec((1, tl, D_LRU), lambda b, s: (b, s, 0)),
        ] + [_const_spec(a.shape) for a in small],
        out_specs=pl.BlockSpec((1, tl, D_LRU), lambda b, s: (b, s, 0)),
        out_shape=jax.ShapeDtypeStruct((B, S, D_LRU), BF16),
        scratch_shapes=[pltpu.VMEM((8, D_LRU), F32), pltpu.VMEM((8, D_LRU), F32)],
        compiler_params=pltpu.CompilerParams(
            dimension_semantics=("arbitrary", "arbitrary"), vmem_limit_bytes=V7X_VMEM_LIMIT_BYTES),
        name="rglru",
    )(lx, lg, *small)


def _tail_kernel(x_ref, ya_ref, yl_ref, wa_ref, wl_ref, g1_ref, b1_ref, wg_ref, wu_ref, wd_ref, g2_ref, b2_ref,
                 out_ref, *, alpha):
    mix = jnp.dot(ya_ref[0], wa_ref[...], preferred_element_type=F32)
    mix = mix + jnp.dot(yl_ref[0], wl_ref[...], preferred_element_type=F32)
    x1 = _layer_norm(alpha * x_ref[0] + mix, g1_ref[...], b1_ref[...])

    xb = x1.astype(BF16)
    gate = jnp.dot(xb, wg_ref[...], preferred_element_type=F32)
    up = jnp.dot(xb, wu_ref[...], preferred_element_type=F32)
    hid = (jax.nn.silu(gate) * up).astype(BF16)
    ffn = jnp.dot(hid, wd_ref[...], preferred_element_type=F32)
    out_ref[0] = _layer_norm(alpha * x1 + ffn, g2_ref[...], b2_ref[...])


def _layer_tail(x, y_att, y_lru, wa, wl, g1, b1, wg, wu, wd, g2, b2, *, ts, alpha):
    B, S, D = x.shape
    consts = [wa, wl, g1, b1, wg, wu, wd, g2, b2]
    return pl.pallas_call(
        functools.partial(_tail_kernel, alpha=alpha),
        grid=(B, S // ts),
        in_specs=[
            pl.BlockSpec((1, ts, D), lambda b_, s: (b_, s, 0)),
            pl.BlockSpec((1, ts, D_ATT), lambda b_, s: (b_, s, 0)),
            pl.BlockSpec((1, ts, D_LRU), lambda b_, s: (b_, s, 0)),
        ] + [_const_spec(c.shape) for c in consts],
        out_specs=pl.BlockSpec((1, ts, D), lambda b_, s: (b_, s, 0)),
        out_shape=jax.ShapeDtypeStruct(x.shape, F32),
        compiler_params=pltpu.CompilerParams(
            dimension_semantics=("arbitrary", "arbitrary"), vmem_limit_bytes=V7X_VMEM_LIMIT_BYTES),
        name="layertail",
    )(x, y_att, y_lru, *consts)


def _pair_heads(fox, moba):
    d = fox.shape[0]
    both = jnp.stack([fox.reshape(d, N_FOX, HEAD_DIM), moba.reshape(d, N_MOBA, HEAD_DIM)], axis=2)
    return both.reshape(d, N_FOX * AUG)


def _transpose_kernel(w_ref, out_ref):
    out_ref[...] = w_ref[...].T.astype(out_ref.dtype)


def _transpose_to_bf16(w, *, tn=256):
    d, n = w.shape
    return pl.pallas_call(
        _transpose_kernel,
        grid=(n // tn,),
        in_specs=[pl.BlockSpec((d, tn), lambda j: (0, j))],
        out_specs=pl.BlockSpec((tn, d), lambda j: (j, 0)),
        out_shape=jax.ShapeDtypeStruct((n, d), BF16),
        compiler_params=pltpu.CompilerParams(
            dimension_semantics=("arbitrary",), vmem_limit_bytes=V7X_VMEM_LIMIT_BYTES),
        name="wtranspose",
    )(w)


def _layer_weights(w_in_l):
    sizes = [N_FOX * HEAD_DIM] * 3 + [N_FOX] + [N_MOBA * HEAD_DIM] * 3 + [D_LRU] * 2
    splits = [sum(sizes[:i + 1]) for i in range(len(sizes) - 1)]
    fq, fk, fv, ff, mq, mk, mv, lx, lg = jnp.split(w_in_l, splits, axis=1)
    wt = _transpose_to_bf16(jnp.concatenate([fq, mq, fv, mv], axis=1))
    ffp = jnp.pad(ff, ((0, 0), (0, AUG - N_FOX)))
    wn = jnp.concatenate([_pair_heads(fk, mk), ffp, lx, lg], axis=1).astype(BF16)
    return wt, wn


def _block_diag(w):
    g, n, _ = w.shape
    eye = jnp.eye(g, dtype=w.dtype)
    return (eye[:, None, :, None] * w[:, :, None, :]).reshape(g * n, g * n)


def _fox_placement():
    place = jnp.zeros((N_SPLIT, AUG, N_FOX, AUG), F32)
    for c in range(N_SPLIT):
        for h in range(N_FOX):
            place = place.at[c, h, h, HEAD_DIM + c].set(1.0)
    return place.reshape(N_SPLIT * AUG, N_FOX * AUG).astype(BF16)


def _tiles(S):
    def fit(n, want):
        t = min(n, want)
        while n % t:
            t //= 2
        return t

    return dict(attn=fit(S, 512), lru=fit(S, 256), tail=fit(S, 512))


def kernel(x, w_in, b_fgate, conv_w, conv_b, w_rgate, b_rgate, w_igate, b_igate, lru_lambda, out_norm_g, w_out,
           ln1_g, ln1_b, w_ffn_gate, w_ffn_up, w_ffn_down, ln2_g, ln2_b):
    B, S, D = x.shape
    depth = w_in.shape[0]
    assert S % MOBA_BLOCK == 0 and S // MOBA_BLOCK <= AUG - HEAD_DIM
    alpha = (2 * depth) ** 0.25
    t = _tiles(S)
    place = _fox_placement()
    ones_bd = _block_diag(jnp.ones((D_LRU // LRU_GROUP, LRU_GROUP, LRU_GROUP), BF16))
    row = lambda v: v.reshape(1, -1).astype(F32)

    for l in range(depth):
        wt, wn = _layer_weights(w_in[l])
        bf_pad = jnp.pad(b_fgate[l], (0, AUG - N_FOX)).reshape(1, AUG)
        qT_aug, k_aug, vT, lx, lg = _inproj(x, wt, wn, bf_pad, place)

        gain = out_norm_g[l]
        nh = ATTN_HEADS_PER_STEP
        y_att = _attention(qT_aug, k_aug, vT, gain[:D_ATT].reshape(N_ATT // nh, 1, nh * HEAD_DIM),
                           tq=t["attn"], nh=nh)

        y_lru = _lru(lx, lg, conv_w[l], row(conv_b[l]), _block_diag(w_rgate[l]).astype(BF16), row(b_rgate[l]),
                     _block_diag(w_igate[l]).astype(BF16), row(b_igate[l]), row(lru_lambda[l]),
                     row(gain[D_ATT:]), ones_bd, tl=t["lru"])

        wo = w_out[l].astype(BF16)
        x = _layer_tail(x, y_att, y_lru, wo[:D_ATT], wo[D_ATT:], row(ln1_g[l]), row(ln1_b[l]),
                        w_ffn_gate[l].astype(BF16), w_ffn_up[l].astype(BF16), w_ffn_down[l].astype(BF16),
                        row(ln2_g[l]), row(ln2_b[l]), ts=t["tail"], alpha=alpha)
    return x
```

```python
import functools
import math

import jax
import jax.numpy as jnp
from jax import lax
from jax.experimental import pallas as pl
from jax.experimental.pallas import tpu as pltpu

F32 = jnp.float32
BF16 = jnp.bfloat16

HEAD_DIM = 64
AUG = 128
N_FOX = 6
N_MOBA = 6
N_ATT = N_FOX + N_MOBA
D_ATT = N_ATT * HEAD_DIM
D_LRU = 256
LRU_GROUP = 64
MOBA_BLOCK = 256
MOBA_TOPK = 3
CONV_WIDTH = 4
LRU_C = 8.0
LN_EPS = 1e-5
RMS_EPS = 1e-6
NEG = -1e30
LOG2E = math.log2(math.e)
QK_SCALE = LOG2E / math.sqrt(HEAD_DIM)
DENOM_ROWS = 16
N_SPLIT = 3
assert N_FOX == N_MOBA
ATTN_HEADS_PER_STEP = 6
ATTN_CHUNKS_PER_ITER = 8
ATTN_ISSUE_ORDER = "SU" * ATTN_HEADS_PER_STEP

V7X_VMEM_BYTES = 64 * 1024 * 1024
V7X_VMEM_LIMIT_BYTES = V7X_VMEM_BYTES - 10 * 1024 * 1024

NT_DIMS = (((1,), (1,)), ((), ()))


def _split3(x):
    hi = x.astype(BF16)
    r1 = x - hi.astype(F32)
    mid = r1.astype(BF16)
    lo = (r1 - mid.astype(F32)).astype(BF16)
    return hi, mid, lo


def _layer_norm(z, g, b):
    mu = jnp.mean(z, axis=-1, keepdims=True)
    zc = z - mu
    var = jnp.mean(zc * zc, axis=-1, keepdims=True)
    return zc * lax.rsqrt(var + LN_EPS) * g + b


def _const_spec(shape):
    nd = len(shape)
    return pl.BlockSpec(shape, lambda *_: (0,) * nd, pipeline_mode=pl.Buffered(1))


def _moba_block_bias(gate, cur):
    nblk, tq = gate.shape
    blk = lax.broadcasted_iota(jnp.int32, (nblk, tq), 0)
    past = blk < cur
    g = jnp.where(past, gate, NEG)
    chosen = jnp.zeros((nblk, tq), jnp.bool_)
    for _ in range(MOBA_TOPK):
        top = jnp.max(g, axis=0, keepdims=True)
        first = jnp.min(jnp.where(g == top, blk, nblk), axis=0, keepdims=True)
        pick = blk == first
        chosen = chosen | pick
        g = jnp.where(pick, -jnp.inf, g)
    allowed = (chosen & past) | (blk == cur)
    return jnp.where(allowed, 0.0, NEG)


def _inproj_kernel(x_ref, wt_ref, wn_ref, bf_ref, place_ref, qT_ref, k_ref, vT_ref, lx_ref, lg_ref,
                   tri_ref, fsum_ref, kmean_ref):
    ts = MOBA_BLOCK
    si = pl.program_id(1)

    @pl.when(si == 0)
    def _():
        r = lax.broadcasted_iota(jnp.int32, (ts, ts), 0)
        c = lax.broadcasted_iota(jnp.int32, (ts, ts), 1)
        tri_ref[...] = (c <= r).astype(BF16)
        fsum_ref[...] = jnp.zeros_like(fsum_ref)
        kmean_ref[...] = jnp.zeros_like(kmean_ref)

    xb = x_ref[0].astype(BF16)
    t = lax.dot_general(wt_ref[...], xb, NT_DIMS, preferred_element_type=F32)
    n = jnp.dot(xb, wn_ref[...], preferred_element_type=F32)
    base = N_FOX * AUG
    vT_ref[0] = t[D_ATT:, :].astype(BF16)
    lx_ref[0] = n[:, base + AUG:base + AUG + D_LRU]
    lg_ref[0] = n[:, base + AUG + D_LRU:base + AUG + 2 * D_LRU]

    log_f = jax.nn.log_sigmoid(n[:, base:base + AUG] + bf_ref[...])
    cum = fsum_ref[0:1, :]
    for piece in _split3(log_f):
        cum = cum + jnp.dot(tri_ref[...], piece, preferred_element_type=F32)
    fsum_ref[...] = jnp.broadcast_to(cum[ts - 1:ts, :], fsum_ref.shape)
    pieces = jnp.concatenate(_split3(-LOG2E * cum), axis=1)
    fox_bias = jnp.dot(pieces, place_ref[...], preferred_element_type=F32)

    lane = lax.broadcasted_iota(jnp.int32, (ts, AUG), 1)
    low = lane < HEAD_DIM
    onehot = jnp.where(lane == si, 1.0, 0.0)
    for g in range(N_FOX):
        grp = slice(g * AUG, (g + 1) * AUG)
        kk = n[:, grp]
        k_ref[0, :, grp] = jnp.where(low, kk, fox_bias[:, grp]).astype(BF16)
        k_ref[0, :, (N_FOX + g) * AUG:(N_FOX + g + 1) * AUG] = jnp.where(low, onehot, kk).astype(BF16)

    kmean = kmean_ref[...]
    row = lax.broadcasted_iota(jnp.int32, (HEAD_DIM, ts), 0)
    fox_bias_rows = jnp.where(row < N_SPLIT, 1.0, 0.0).astype(BF16)
    for j in range(N_ATT):
        q = (t[j * HEAD_DIM:(j + 1) * HEAD_DIM, :] * QK_SCALE).astype(BF16)
        if j < N_FOX:
            halves = (q, fox_bias_rows)
        else:
            km = kmean[:, (j - N_FOX) * AUG + HEAD_DIM:(j - N_FOX + 1) * AUG].astype(BF16)
            gate = jnp.dot(km, q, preferred_element_type=F32)
            halves = (_moba_block_bias(gate, si).astype(BF16), q)
        qT_ref[0, j * AUG:j * AUG + HEAD_DIM, :] = halves[0]
        qT_ref[0, j * AUG + HEAD_DIM:(j + 1) * AUG, :] = halves[1]

    block_mean = jnp.mean(n[:, 0:base], axis=0, keepdims=True)
    blk_row = lax.broadcasted_iota(jnp.int32, kmean.shape, 0)
    kmean_ref[...] = jnp.where(blk_row == si, block_mean, kmean)


def _inproj(x, wt, wn, bf_pad, place):
    B, S, D = x.shape
    ts = MOBA_BLOCK
    out_shape = (
        jax.ShapeDtypeStruct((B, N_ATT * AUG, S), BF16),
        jax.ShapeDtypeStruct((B, S, N_ATT * AUG), BF16),
        jax.ShapeDtypeStruct((B, D_ATT, S), BF16),
        jax.ShapeDtypeStruct((B, S, D_LRU), F32),
        jax.ShapeDtypeStruct((B, S, D_LRU), F32),
    )
    return pl.pallas_call(
        _inproj_kernel,
        grid=(B, S // ts),
        in_specs=[
            pl.BlockSpec((1, ts, D), lambda b, s: (b, s, 0)),
            _const_spec(wt.shape), _const_spec(wn.shape), _const_spec(bf_pad.shape), _const_spec(place.shape),
        ],
        out_specs=(
            pl.BlockSpec((1, N_ATT * AUG, ts), lambda b, s: (b, 0, s)),
            pl.BlockSpec((1, ts, N_ATT * AUG), lambda b, s: (b, s, 0)),
            pl.BlockSpec((1, D_ATT, ts), lambda b, s: (b, 0, s)),
            pl.BlockSpec((1, ts, D_LRU), lambda b, s: (b, s, 0)),
            pl.BlockSpec((1, ts, D_LRU), lambda b, s: (b, s, 0)),
        ),
        out_shape=out_shape,
        scratch_shapes=[
            pltpu.VMEM((ts, ts), BF16),
            pltpu.VMEM((8, AUG), F32),
            pltpu.VMEM((HEAD_DIM, N_FOX * AUG), F32),
        ],
        compiler_params=pltpu.CompilerParams(
            dimension_semantics=("arbitrary", "arbitrary"), vmem_limit_bytes=V7X_VMEM_LIMIT_BYTES),
        name="inproj",
    )(x, wt, wn, bf_pad, place)


def _attn_kernel(qT_ref, k_ref, vT_ref, g_ref, out_ref, sa_ref, sb_ref, mxa_ref, mxb_ref, m_ref, acc_ref,
                 *, tq, kw, nh):
    qi = pl.program_id(2)
    heads = range(nh)
    ones = jnp.ones((DENOM_ROWS, kw), BF16)
    bufs = ((sa_ref, mxa_ref), (sb_ref, mxb_ref))

    def score_head(c, hh, s_ref, mx_ref):
        off = pl.multiple_of(c * kw, kw)
        s = jnp.dot(k_ref[0, pl.ds(off, kw), hh * AUG:(hh + 1) * AUG], qT_ref[0, hh * AUG:(hh + 1) * AUG, :],
                    preferred_element_type=F32)
        s_ref[hh] = s
        mx_ref[hh] = jnp.max(s, axis=0, keepdims=True)

    def update_head(c, hh, s_ref, mx_ref, masked):
        off = pl.multiple_of(c * kw, kw)
        if masked:
            kpos = off + lax.broadcasted_iota(jnp.int32, (kw, tq), 0)
            qpos = qi * tq + lax.broadcasted_iota(jnp.int32, (kw, tq), 1)
            s = jnp.where(kpos <= qpos, s_ref[hh], NEG)
            mx = jnp.max(s, axis=0, keepdims=True)
        else:
            s = s_ref[hh]
            mx = mx_ref[hh]
        m = m_ref[hh]
        m_new = jnp.maximum(m, mx)
        alpha = jnp.exp2(m - m_new)
        p = jnp.exp2(s - m_new).astype(BF16)
        vT = vT_ref[0, hh * HEAD_DIM:(hh + 1) * HEAD_DIM, pl.ds(off, kw)]
        v_ones = jnp.concatenate([vT, ones], axis=0)
        acc_ref[hh] = alpha * acc_ref[hh] + jnp.dot(v_ones, p, preferred_element_type=F32)
        m_ref[hh] = m_new

    def stage(c, parity, *, prefetch=True, masked=False):
        todo = {"S": iter(heads), "U": iter(heads)}
        for step in ATTN_ISSUE_ORDER:
            hh = next(todo[step])
            if step == "U":
                update_head(c, hh, *bufs[parity], masked)
            elif prefetch:
                score_head(c + 1, hh, *bufs[1 - parity])

    def past_chunks(first, count):
        for t in range(count):
            stage(first + t, t % 2)

    m_ref[...] = jnp.full(m_ref.shape, -jnp.inf, F32)
    acc_ref[...] = jnp.zeros(acc_ref.shape, F32)
    for hh in heads:
        score_head(0, hh, *bufs[0])

    pairs_per_iter = ATTN_CHUNKS_PER_ITER // 2

    @pl.loop(0, qi // pairs_per_iter)
    def _(j):
        past_chunks(j * ATTN_CHUNKS_PER_ITER, ATTN_CHUNKS_PER_ITER)

    for r in range(1, pairs_per_iter):
        @pl.when(qi % pairs_per_iter == r)
        def _():
            past_chunks((qi - r) * 2, 2 * r)

    stage(2 * qi, 0, masked=True)
    stage(2 * qi + 1, 1, prefetch=False, masked=True)

    normed = []
    for hh in heads:
        acc = acc_ref[hh]
        o = acc[:HEAD_DIM] * (1.0 / acc[HEAD_DIM:HEAD_DIM + 1])
        normed.append(o * lax.rsqrt(jnp.mean(o * o, axis=0, keepdims=True) + RMS_EPS))
    oT = jnp.concatenate(normed, axis=0)
    out_ref[0] = (oT.T * g_ref[0]).astype(out_ref.dtype)


def _attention(qT_aug, k_aug, vT, gain, *, tq, nh):
    B, S, _ = k_aug.shape
    kw = tq // 2
    score_buf = pltpu.VMEM((nh, kw, tq), F32)
    max_buf = pltpu.VMEM((nh, 1, tq), F32)
    acc_buf = pltpu.VMEM((nh, HEAD_DIM + DENOM_ROWS, tq), F32)
    return pl.pallas_call(
        functools.partial(_attn_kernel, tq=tq, kw=kw, nh=nh),
        grid=(B, N_ATT // nh, S // tq),
        in_specs=[
            pl.BlockSpec((1, nh * AUG, tq), lambda b, p, i: (b, p, i)),
            pl.BlockSpec((1, S, nh * AUG), lambda b, p, i: (b, 0, p), pipeline_mode=pl.Buffered(1)),
            pl.BlockSpec((1, nh * HEAD_DIM, S), lambda b, p, i: (b, p, 0), pipeline_mode=pl.Buffered(1)),
            pl.BlockSpec((1, 1, nh * HEAD_DIM), lambda b, p, i: (p, 0, 0)),
        ],
        out_specs=pl.BlockSpec((1, tq, nh * HEAD_DIM), lambda b, p, i: (b, i, p)),
        out_shape=jax.ShapeDtypeStruct((B, S, D_ATT), BF16),
        scratch_shapes=[score_buf, score_buf, max_buf, max_buf, max_buf, acc_buf],
        compiler_params=pltpu.CompilerParams(
            dimension_semantics=("arbitrary", "arbitrary", "arbitrary"),
            vmem_limit_bytes=V7X_VMEM_LIMIT_BYTES),
        name="attention",
    )(qT_aug, k_aug, vT, gain)


def _lru_kernel(lx_ref, lg_ref, cw_ref, cb_ref, wr_ref, br_ref, wi_ref, bi_ref, lam_ref, g_ref, ones_ref,
                out_ref, tail_ref, h_ref, *, tl):
    @pl.when(pl.program_id(1) == 0)
    def _():
        tail_ref[...] = jnp.zeros_like(tail_ref)
        h_ref[...] = jnp.zeros_like(h_ref)

    x = lx_ref[0]
    xx = jnp.concatenate([tail_ref[...], x], axis=0)
    tail_ref[...] = x[tl - 8:tl, :]
    xc = cb_ref[...] + x * cw_ref[CONV_WIDTH - 1:CONV_WIDTH, :]
    for d in range(1, CONV_WIDTH):
        xc = xc + xx[8 - d:8 - d + tl, :] * cw_ref[CONV_WIDTH - 1 - d:CONV_WIDTH - d, :]

    xcb = xc.astype(BF16)
    r = jax.nn.sigmoid(jnp.dot(xcb, wr_ref[...], preferred_element_type=F32) + br_ref[...])
    i = jax.nn.sigmoid(jnp.dot(xcb, wi_ref[...], preferred_element_type=F32) + bi_ref[...])
    log_a = LRU_C * r * jax.nn.log_sigmoid(lam_ref[...])
    a = jnp.exp(log_a)
    u = jnp.sqrt(-jnp.tanh(log_a) * (a * a + 1.0)) * (i * xc)

    pos = lax.broadcasted_iota(jnp.int32, (tl, D_LRU), 0)
    d = 1
    while d < tl:
        keep = pos >= d
        a_prev = jnp.where(keep, pltpu.roll(a, d, 0), 1.0)
        u_prev = jnp.where(keep, pltpu.roll(u, d, 0), 0.0)
        u = a * u_prev + u
        a = a * a_prev
        d *= 2
    h = u + a * h_ref[0:1, :]
    h_ref[...] = jnp.broadcast_to(h[tl - 1:tl, :], h_ref.shape)

    y = jax.nn.gelu(lg_ref[0]) * h
    sq_hi, sq_mid, _ = _split3(y * y)
    ms = (jnp.dot(sq_hi, ones_ref[...], preferred_element_type=F32)
          + jnp.dot(sq_mid, ones_ref[...], preferred_element_type=F32)) * (1.0 / LRU_GROUP)
    out_ref[0] = (y * lax.rsqrt(ms + RMS_EPS) * g_ref[...]).astype(out_ref.dtype)


def _lru(lx, lg, cw, cb, wr, br, wi, bi, lam, gain, ones_bd, *, tl):
    B, S, _ = lx.shape
    small = [cw, cb, wr, br, wi, bi, lam, gain, ones_bd]
    return pl.pallas_call(
        functools.partial(_lru_kernel, tl=tl),
        grid=(B, S // tl),
        in_specs=[
            pl.BlockSpec((1, tl, D_LRU), lambda b, s: (b, s, 0)),
            pl.BlockSpec((1, tl, D_LRU), lambda b, s: (b, s, 0)),
        ] + [_const_spec(a.shape) for a in small],
        out_specs=pl.BlockSpec((1, tl, D_LRU), lambda b, s: (b, s, 0)),
        out_shape=jax.ShapeDtypeStruct((B, S, D_LRU), BF16),
        scratch_shapes=[pltpu.VMEM((8, D_LRU), F32), pltpu.VMEM((8, D_LRU), F32)],
        compiler_params=pltpu.CompilerParams(
            dimension_semantics=("arbitrary", "arbitrary"), vmem_limit_bytes=V7X_VMEM_LIMIT_BYTES),
        name="rglru",
    )(lx, lg, *small)


def _tail_kernel(x_ref, ya_ref, yl_ref, wa_ref, wl_ref, g1_ref, b1_ref, wg_ref, wu_ref, wd_ref, g2_ref, b2_ref,
                 out_ref, *, alpha):
    mix = jnp.dot(ya_ref[0], wa_ref[...], preferred_element_type=F32)
    mix = mix + jnp.dot(yl_ref[0], wl_ref[...], preferred_element_type=F32)
    x1 = _layer_norm(alpha * x_ref[0] + mix, g1_ref[...], b1_ref[...])

    xb = x1.astype(BF16)
    gate = jnp.dot(xb, wg_ref[...], preferred_element_type=F32)
    up = jnp.dot(xb, wu_ref[...], preferred_element_type=F32)
    hid = (jax.nn.silu(gate) * up).astype(BF16)
    ffn = jnp.dot(hid, wd_ref[...], preferred_element_type=F32)
    out_ref[0] = _layer_norm(alpha * x1 + ffn, g2_ref[...], b2_ref[...])


def _layer_tail(x, y_att, y_lru, wa, wl, g1, b1, wg, wu, wd, g2, b2, *, ts, alpha):
    B, S, D = x.shape
    consts = [wa, wl, g1, b1, wg, wu, wd, g2, b2]
    return pl.pallas_call(
        functools.partial(_tail_kernel, alpha=alpha),
        grid=(B, S // ts),
        in_specs=[
            pl.BlockSpec((1, ts, D), lambda b_, s: (b_, s, 0)),
            pl.BlockSpec((1, ts, D_ATT), lambda b_, s: (b_, s, 0)),
            pl.BlockSpec((1, ts, D_LRU), lambda b_, s: (b_, s, 0)),
        ] + [_const_spec(c.shape) for c in consts],
        out_specs=pl.BlockSpec((1, ts, D), lambda b_, s: (b_, s, 0)),
        out_shape=jax.ShapeDtypeStruct(x.shape, F32),
        compiler_params=pltpu.CompilerParams(
            dimension_semantics=("arbitrary", "arbitrary"), vmem_limit_bytes=V7X_VMEM_LIMIT_BYTES),
        name="layertail",
    )(x, y_att, y_lru, *consts)


def _pair_heads(fox, moba):
    d = fox.shape[0]
    both = jnp.stack([fox.reshape(d, N_FOX, HEAD_DIM), moba.reshape(d, N_MOBA, HEAD_DIM)], axis=2)
    return both.reshape(d, N_FOX * AUG)


def _transpose_kernel(w_ref, out_ref):
    out_ref[...] = w_ref[...].T.astype(out_ref.dtype)


def _transpose_to_bf16(w, *, tn=256):
    d, n = w.shape
    return pl.pallas_call(
        _transpose_kernel,
        grid=(n // tn,),
        in_specs=[pl.BlockSpec((d, tn), lambda j: (0, j))],
        out_specs=pl.BlockSpec((tn, d), lambda j: (j, 0)),
        out_shape=jax.ShapeDtypeStruct((n, d), BF16),
        compiler_params=pltpu.CompilerParams(
            dimension_semantics=("arbitrary",), vmem_limit_bytes=V7X_VMEM_LIMIT_BYTES),
        name="wtranspose",
    )(w)


def _layer_weights(w_in_l):
    sizes = [N_FOX * HEAD_DIM] * 3 + [N_FOX] + [N_MOBA * HEAD_DIM] * 3 + [D_LRU] * 2
    splits = [sum(sizes[:i + 1]) for i in range(len(sizes) - 1)]
    fq, fk, fv, ff, mq, mk, mv, lx, lg = jnp.split(w_in_l, splits, axis=1)
    wt = _transpose_to_bf16(jnp.concatenate([fq, mq, fv, mv], axis=1))
    ffp = jnp.pad(ff, ((0, 0), (0, AUG - N_FOX)))
    wn = jnp.concatenate([_pair_heads(fk, mk), ffp, lx, lg], axis=1).astype(BF16)
    return wt, wn


def _block_diag(w):
    g, n, _ = w.shape
    eye = jnp.eye(g, dtype=w.dtype)
    return (eye[:, None, :, None] * w[:, :, None, :]).reshape(g * n, g * n)


def _fox_placement():
    place = jnp.zeros((N_SPLIT, AUG, N_FOX, AUG), F32)
    for c in range(N_SPLIT):
        for h in range(N_FOX):
            place = place.at[c, h, h, HEAD_DIM + c].set(1.0)
    return place.reshape(N_SPLIT * AUG, N_FOX * AUG).astype(BF16)


def _tiles(S):
    def fit(n, want):
        t = min(n, want)
        while n % t:
            t //= 2
        return t

    return dict(attn=fit(S, 512), lru=fit(S, 256), tail=fit(S, 512))


def kernel(x, w_in, b_fgate, conv_w, conv_b, w_rgate, b_rgate, w_igate, b_igate, lru_lambda, out_norm_g, w_out,
           ln1_g, ln1_b, w_ffn_gate, w_ffn_up, w_ffn_down, ln2_g, ln2_b):
    B, S, D = x.shape
    depth = w_in.shape[0]
    assert S % MOBA_BLOCK == 0 and S // MOBA_BLOCK <= AUG - HEAD_DIM
    alpha = (2 * depth) ** 0.25
    t = _tiles(S)
    place = _fox_placement()
    ones_bd = _block_diag(jnp.ones((D_LRU // LRU_GROUP, LRU_GROUP, LRU_GROUP), BF16))
    row = lambda v: v.reshape(1, -1).astype(F32)

    for l in range(depth):
        wt, wn = _layer_weights(w_in[l])
        bf_pad = jnp.pad(b_fgate[l], (0, AUG - N_FOX)).reshape(1, AUG)
        qT_aug, k_aug, vT, lx, lg = _inproj(x, wt, wn, bf_pad, place)

        gain = out_norm_g[l]
        nh = ATTN_HEADS_PER_STEP
        y_att = _attention(qT_aug, k_aug, vT, gain[:D_ATT].reshape(N_ATT // nh, 1, nh * HEAD_DIM),
                           tq=t["attn"], nh=nh)

        y_lru = _lru(lx, lg, conv_w[l], row(conv_b[l]), _block_diag(w_rgate[l]).astype(BF16), row(b_rgate[l]),
                     _block_diag(w_igate[l]).astype(BF16), row(b_igate[l]), row(lru_lambda[l]),
                     row(gain[D_ATT:]), ones_bd, tl=t["lru"])

        wo = w_out[l].astype(BF16)
        x = _layer_tail(x, y_att, y_lru, wo[:D_ATT], wo[D_ATT:], row(ln1_g[l]), row(ln1_b[l]),
                        w_ffn_gate[l].astype(BF16), w_ffn_up[l].astype(BF16), w_ffn_down[l].astype(BF16),
                        row(ln2_g[l]), row(ln2_b[l]), ts=t["tail"], alpha=alpha)
    return x
```

```python
import functools
import math

import jax
import jax.numpy as jnp
from jax import lax
from jax.experimental import pallas as pl
from jax.experimental.pallas import tpu as pltpu

F32 = jnp.float32
BF16 = jnp.bfloat16

HEAD_DIM = 64
AUG = 128
N_FOX = 6
N_MOBA = 6
N_ATT = N_FOX + N_MOBA
D_ATT = N_ATT * HEAD_DIM
D_LRU = 256
LRU_GROUP = 64
MOBA_BLOCK = 256
MOBA_TOPK = 3
CONV_WIDTH = 4
LRU_C = 8.0
LRU_ROWS = 256
LRU_CARRY_ROWS = 8
LN_EPS = 1e-5
RMS_EPS = 1e-6
NEG = -1e30
LOG2E = math.log2(math.e)
QK_SCALE = LOG2E / math.sqrt(HEAD_DIM)
DENOM_ROWS = 16
N_SPLIT = 3
assert N_FOX == N_MOBA
ATTN_HEADS_PER_STEP = 4
ATTN_CHUNKS_PER_ITER = 8
ATTN_ISSUE_ORDER = "SUSUSUSU"

V7X_VMEM_LIMIT_BYTES = 48 * 1024 * 1024

NT_DIMS = (((1,), (1,)), ((), ()))


def _split3(x):
    hi = x.astype(BF16)
    r1 = x - hi.astype(F32)
    mid = r1.astype(BF16)
    lo = (r1 - mid.astype(F32)).astype(BF16)
    return hi, mid, lo


def _layer_norm(z, g, b):
    mu = jnp.mean(z, axis=-1, keepdims=True)
    zc = z - mu
    var = jnp.mean(zc * zc, axis=-1, keepdims=True)
    return zc * lax.rsqrt(var + LN_EPS) * g + b


def _const_spec(shape):
    nd = len(shape)
    return pl.BlockSpec(shape, lambda *_: (0,) * nd, pipeline_mode=pl.Buffered(1))


def _moba_block_bias(gate, cur):
    nblk, tq = gate.shape
    blk = lax.broadcasted_iota(jnp.int32, (nblk, tq), 0)
    past = blk < cur
    g = jnp.where(past, gate, NEG)
    chosen = jnp.zeros((nblk, tq), jnp.bool_)
    for _ in range(MOBA_TOPK):
        top = jnp.max(g, axis=0, keepdims=True)
        first = jnp.min(jnp.where(g == top, blk, nblk), axis=0, keepdims=True)
        pick = blk == first
        chosen = chosen | pick
        g = jnp.where(pick, -jnp.inf, g)
    allowed = (chosen & past) | (blk == cur)
    return jnp.where(allowed, 0.0, NEG)


def _inproj_kernel(x_ref, wt_ref, wn_ref, bf_ref, place_ref, qT_ref, k_ref, vT_ref, lx_ref, lg_ref,
                   tri_ref, fsum_ref, kmean_ref):
    ts = MOBA_BLOCK
    si = pl.program_id(1)

    @pl.when(si == 0)
    def _():
        r = lax.broadcasted_iota(jnp.int32, (ts, ts), 0)
        c = lax.broadcasted_iota(jnp.int32, (ts, ts), 1)
        tri_ref[...] = (c <= r).astype(BF16)
        fsum_ref[...] = jnp.zeros_like(fsum_ref)
        kmean_ref[...] = jnp.zeros_like(kmean_ref)

    xb = x_ref[0].astype(BF16)
    t = lax.dot_general(wt_ref[...], xb, NT_DIMS, preferred_element_type=F32)
    n = jnp.dot(xb, wn_ref[...], preferred_element_type=F32)
    base = N_FOX * AUG
    vT_ref[0] = t[D_ATT:, :].astype(BF16)
    lx_ref[0] = n[:, base + AUG:base + AUG + D_LRU]
    lg_ref[0] = n[:, base + AUG + D_LRU:base + AUG + 2 * D_LRU]

    log_f = jax.nn.log_sigmoid(n[:, base:base + AUG] + bf_ref[...])
    cum = fsum_ref[0:1, :]
    for piece in _split3(log_f):
        cum = cum + jnp.dot(tri_ref[...], piece, preferred_element_type=F32)
    fsum_ref[...] = jnp.broadcast_to(cum[ts - 1:ts, :], fsum_ref.shape)
    pieces = jnp.concatenate(_split3(-LOG2E * cum), axis=1)
    fox_bias = jnp.dot(pieces, place_ref[...], preferred_element_type=F32)

    lane = lax.broadcasted_iota(jnp.int32, (ts, AUG), 1)
    low = lane < HEAD_DIM
    onehot = jnp.where(lane == si, 1.0, 0.0)
    for g in range(N_FOX):
        grp = slice(g * AUG, (g + 1) * AUG)
        kk = n[:, grp]
        k_ref[0, :, grp] = jnp.where(low, kk, fox_bias[:, grp]).astype(BF16)
        k_ref[0, :, (N_FOX + g) * AUG:(N_FOX + g + 1) * AUG] = jnp.where(low, onehot, kk).astype(BF16)

    kmean = kmean_ref[...]
    row = lax.broadcasted_iota(jnp.int32, (HEAD_DIM, ts), 0)
    fox_bias_rows = jnp.where(row < N_SPLIT, 1.0, 0.0).astype(BF16)
    for j in range(N_ATT):
        q = (t[j * HEAD_DIM:(j + 1) * HEAD_DIM, :] * QK_SCALE).astype(BF16)
        if j < N_FOX:
            halves = (q, fox_bias_rows)
        else:
            km = kmean[:, (j - N_FOX) * AUG + HEAD_DIM:(j - N_FOX + 1) * AUG].astype(BF16)
            gate = jnp.dot(km, q, preferred_element_type=F32)
            halves = (_moba_block_bias(gate, si).astype(BF16), q)
        qT_ref[0, j * AUG:j * AUG + HEAD_DIM, :] = halves[0]
        qT_ref[0, j * AUG + HEAD_DIM:(j + 1) * AUG, :] = halves[1]

    block_mean = jnp.mean(n[:, 0:base], axis=0, keepdims=True)
    blk_row = lax.broadcasted_iota(jnp.int32, kmean.shape, 0)
    kmean_ref[...] = jnp.where(blk_row == si, block_mean, kmean)


def _inproj(x, wt, wn, bf_pad, place):
    B, S, D = x.shape
    ts = MOBA_BLOCK
    out_shape = (
        jax.ShapeDtypeStruct((B, N_ATT * AUG, S), BF16),
        jax.ShapeDtypeStruct((B, S, N_ATT * AUG), BF16),
        jax.ShapeDtypeStruct((B, D_ATT, S), BF16),
        jax.ShapeDtypeStruct((B, S, D_LRU), F32),
        jax.ShapeDtypeStruct((B, S, D_LRU), F32),
    )
    return pl.pallas_call(
        _inproj_kernel,
        grid=(B, S // ts),
        in_specs=[
            pl.BlockSpec((1, ts, D), lambda b, s: (b, s, 0)),
            _const_spec(wt.shape), _const_spec(wn.shape), _const_spec(bf_pad.shape), _const_spec(place.shape),
        ],
        out_specs=(
            pl.BlockSpec((1, N_ATT * AUG, ts), lambda b, s: (b, 0, s)),
            pl.BlockSpec((1, ts, N_ATT * AUG), lambda b, s: (b, s, 0)),
            pl.BlockSpec((1, D_ATT, ts), lambda b, s: (b, 0, s)),
            pl.BlockSpec((1, ts, D_LRU), lambda b, s: (b, s, 0)),
            pl.BlockSpec((1, ts, D_LRU), lambda b, s: (b, s, 0)),
        ),
        out_shape=out_shape,
        scratch_shapes=[
            pltpu.VMEM((ts, ts), BF16),
            pltpu.VMEM((LRU_CARRY_ROWS, AUG), F32),
            pltpu.VMEM((HEAD_DIM, N_FOX * AUG), F32),
        ],
        compiler_params=pltpu.CompilerParams(
            dimension_semantics=("arbitrary", "arbitrary"), vmem_limit_bytes=V7X_VMEM_LIMIT_BYTES),
        name="inproj",
    )(x, wt, wn, bf_pad, place)


def _attn_kernel(qT_ref, k_ref, vT_ref, g_ref, out_ref, sa_ref, sb_ref, mxa_ref, mxb_ref, m_ref, acc_ref,
                 *, tq, kw, nh):
    qi = pl.program_id(2)
    heads = range(nh)
    ones = jnp.ones((DENOM_ROWS, kw), BF16)
    bufs = ((sa_ref, mxa_ref), (sb_ref, mxb_ref))

    def score_head(c, hh, s_ref, mx_ref):
        off = pl.multiple_of(c * kw, kw)
        s = jnp.dot(k_ref[0, pl.ds(off, kw), hh * AUG:(hh + 1) * AUG], qT_ref[0, hh * AUG:(hh + 1) * AUG, :],
                    preferred_element_type=F32)
        s_ref[hh] = s
        mx_ref[hh] = jnp.max(s, axis=0, keepdims=True)

    def update_head(c, hh, s_ref, mx_ref, masked):
        off = pl.multiple_of(c * kw, kw)
        if masked:
            kpos = off + lax.broadcasted_iota(jnp.int32, (kw, tq), 0)
            qpos = qi * tq + lax.broadcasted_iota(jnp.int32, (kw, tq), 1)
            s = jnp.where(kpos <= qpos, s_ref[hh], NEG)
            mx = jnp.max(s, axis=0, keepdims=True)
        else:
            s = s_ref[hh]
            mx = mx_ref[hh]
        m = m_ref[hh]
        m_new = jnp.maximum(m, mx)
        alpha = jnp.exp2(m - m_new)
        p = jnp.exp2(s - m_new).astype(BF16)
        vT = vT_ref[0, hh * HEAD_DIM:(hh + 1) * HEAD_DIM, pl.ds(off, kw)]
        v_ones = jnp.concatenate([vT, ones], axis=0)
        acc_ref[hh] = alpha * acc_ref[hh] + jnp.dot(v_ones, p, preferred_element_type=F32)
        m_ref[hh] = m_new

    def stage(c, parity, *, prefetch=True, masked=False):
        todo = {"S": iter(heads), "U": iter(heads)}
        for step in ATTN_ISSUE_ORDER:
            hh = next(todo[step])
            if step == "U":
                update_head(c, hh, *bufs[parity], masked)
            elif prefetch:
                score_head(c + 1, hh, *bufs[1 - parity])

    def past_chunks(first, count):
        for t in range(count):
            stage(first + t, t % 2)

    m_ref[...] = jnp.full(m_ref.shape, -jnp.inf, F32)
    acc_ref[...] = jnp.zeros(acc_ref.shape, F32)
    for hh in heads:
        score_head(0, hh, *bufs[0])

    pairs_per_iter = ATTN_CHUNKS_PER_ITER // 2

    @pl.loop(0, qi // pairs_per_iter)
    def _(j):
        past_chunks(j * ATTN_CHUNKS_PER_ITER, ATTN_CHUNKS_PER_ITER)

    for r in range(1, pairs_per_iter):
        @pl.when(qi % pairs_per_iter == r)
        def _():
            past_chunks((qi - r) * 2, 2 * r)

    stage(2 * qi, 0, masked=True)
    stage(2 * qi + 1, 1, prefetch=False, masked=True)

    normed = []
    for hh in heads:
        acc = acc_ref[hh]
        o = acc[:HEAD_DIM] * (1.0 / acc[HEAD_DIM:HEAD_DIM + 1])
        normed.append(o * lax.rsqrt(jnp.mean(o * o, axis=0, keepdims=True) + RMS_EPS))
    oT = jnp.concatenate(normed, axis=0)
    out_ref[0] = (oT.T * g_ref[0]).astype(out_ref.dtype)


def _attention(qT_aug, k_aug, vT, gain, *, tq, nh):
    B, S, _ = k_aug.shape
    kw = tq // 2
    score_buf = pltpu.VMEM((nh, kw, tq), F32)
    max_buf = pltpu.VMEM((nh, 1, tq), F32)
    acc_buf = pltpu.VMEM((nh, HEAD_DIM + DENOM_ROWS, tq), F32)
    return pl.pallas_call(
        functools.partial(_attn_kernel, tq=tq, kw=kw, nh=nh),
        grid=(B, N_ATT // nh, S // tq),
        in_specs=[
            pl.BlockSpec((1, nh * AUG, tq), lambda b, p, i: (b, p, i)),
            pl.BlockSpec((1, S, nh * AUG), lambda b, p, i: (b, 0, p), pipeline_mode=pl.Buffered(1)),
            pl.BlockSpec((1, nh * HEAD_DIM, S), lambda b, p, i: (b, p, 0), pipeline_mode=pl.Buffered(1)),
            pl.BlockSpec((1, 1, nh * HEAD_DIM), lambda b, p, i: (p, 0, 0)),
        ],
        out_specs=pl.BlockSpec((1, tq, nh * HEAD_DIM), lambda b, p, i: (b, i, p)),
        out_shape=jax.ShapeDtypeStruct((B, S, D_ATT), BF16),
        scratch_shapes=[score_buf, score_buf, max_buf, max_buf, max_buf, acc_buf],
        compiler_params=pltpu.CompilerParams(
            dimension_semantics=("arbitrary", "arbitrary", "arbitrary"),
            vmem_limit_bytes=V7X_VMEM_LIMIT_BYTES),
        name="attention",
    )(qT_aug, k_aug, vT, gain)


def _lru_gates(x, cw_ref, cb_ref, wr_ref, wi_ref, tail_ref):
    tl = x.shape[0]
    xx = jnp.concatenate([tail_ref[...], x], axis=0)
    tail_ref[...] = x[tl - LRU_CARRY_ROWS:tl, :]
    xc = cb_ref[...] + x * cw_ref[CONV_WIDTH - 1:CONV_WIDTH, :]
    for d in range(1, CONV_WIDTH):
        lo = LRU_CARRY_ROWS - d
        xc = xc + xx[lo:lo + tl, :] * cw_ref[CONV_WIDTH - 1 - d:CONV_WIDTH - d, :]
    xcb = xc.astype(BF16)
    return (xc, jnp.dot(xcb, wr_ref[...], preferred_element_type=F32),
            jnp.dot(xcb, wi_ref[...], preferred_element_type=F32))


def _lru_scan(gates, lg, br_ref, bi_ref, lam_ref, h_ref):
    xc, r_pre, i_pre = gates
    tl = xc.shape[0]
    r = jax.nn.sigmoid(r_pre + br_ref[...])
    i = jax.nn.sigmoid(i_pre + bi_ref[...])
    log_a = LRU_C * r * jax.nn.log_sigmoid(lam_ref[...])
    a = jnp.exp(log_a)
    u = jnp.sqrt(-jnp.tanh(log_a) * (a * a + 1.0)) * (i * xc)

    pos = lax.broadcasted_iota(jnp.int32, (tl, D_LRU), 0)
    d = 1
    while d < tl:
        keep = pos >= d
        a_prev = jnp.where(keep, pltpu.roll(a, d, 0), 1.0)
        u_prev = jnp.where(keep, pltpu.roll(u, d, 0), 0.0)
        u = a * u_prev + u
        a = a * a_prev
        d *= 2
    h = u + a * h_ref[0:1, :]
    h_ref[...] = jnp.broadcast_to(h[tl - 1:tl, :], h_ref.shape)
    return jax.nn.gelu(lg) * h


def _lru_norm(y, g_ref, ones_ref):
    sq_hi, sq_mid, _ = _split3(y * y)
    ms = (jnp.dot(sq_hi, ones_ref[...], preferred_element_type=F32)
          + jnp.dot(sq_mid, ones_ref[...], preferred_element_type=F32)) * (1.0 / LRU_GROUP)
    return y * lax.rsqrt(ms + RMS_EPS) * g_ref[...]


def _tail_kernel(x_ref, ya_ref, lx0_ref, lg0_ref, lx_ref, lg_ref, cw_ref, cb_ref, wr_ref, br_ref, wi_ref, bi_ref,
                 lam_ref, gl_ref, ones_ref, wa_ref, wl_ref, g1_ref, b1_ref, wg_ref, wu_ref, wd_ref, g2_ref, b2_ref,
                 out_ref, conv_tail_ref, h_ref, ylru_ref, *, alpha, ts):
    halves = list(range(0, ts, LRU_ROWS))

    def lru_gates(lx_t, r0):
        return _lru_gates(lx_t[0, r0:r0 + LRU_ROWS, :], cw_ref, cb_ref, wr_ref, wi_ref, conv_tail_ref)

    def lru_scan(gates, lg_t, r0):
        return _lru_scan(gates, lg_t[0, r0:r0 + LRU_ROWS, :], br_ref, bi_ref, lam_ref, h_ref)

    def lru_store(ys):
        ylru_ref[...] = jnp.concatenate([_lru_norm(y, gl_ref, ones_ref) for y in ys], axis=0).astype(BF16)

    @pl.when(pl.program_id(1) == 0)
    def _():
        conv_tail_ref[...] = jnp.zeros_like(conv_tail_ref)
        h_ref[...] = jnp.zeros_like(h_ref)
        lru_store([lru_scan(lru_gates(lx0_ref, r0), lg0_ref, r0) for r0 in halves])

    y_lru = ylru_ref[...]
    next_gates = [lru_gates(lx_ref, r0) for r0 in halves]

    mix = jnp.dot(ya_ref[0], wa_ref[...], preferred_element_type=F32)
    mix = mix + jnp.dot(y_lru, wl_ref[...], preferred_element_type=F32)
    x1 = _layer_norm(alpha * x_ref[0] + mix, g1_ref[...], b1_ref[...])
    xb = x1.astype(BF16)

    next_y = [lru_scan(next_gates[0], lg_ref, halves[0])]
    gate = jnp.dot(xb, wg_ref[...], preferred_element_type=F32)
    up = jnp.dot(xb, wu_ref[...], preferred_element_type=F32)
    next_y += [lru_scan(g, lg_ref, r0) for g, r0 in zip(next_gates[1:], halves[1:])]
    hid = (jax.nn.silu(gate) * up).astype(BF16)
    ffn = jnp.dot(hid, wd_ref[...], preferred_element_type=F32)
    out_ref[0] = _layer_norm(alpha * x1 + ffn, g2_ref[...], b2_ref[...])

    lru_store(next_y)


def _layer_tail(x, y_att, lx, lg, lru_params, wa, wl, g1, b1, wg, wu, wd, g2, b2, *, ts, alpha):
    B, S, D = x.shape
    last = S // ts - 1
    consts = [*lru_params, wa, wl, g1, b1, wg, wu, wd, g2, b2]
    rows = lambda width: pl.BlockSpec((1, ts, width), lambda b_, s: (b_, s, 0))
    next_rows = lambda width: pl.BlockSpec((1, ts, width), lambda b_, s: (b_, jnp.minimum(s + 1, last), 0))
    return pl.pallas_call(
        functools.partial(_tail_kernel, alpha=alpha, ts=ts),
        grid=(B, S // ts),
        in_specs=[rows(D), rows(D_ATT), rows(D_LRU), rows(D_LRU), next_rows(D_LRU), next_rows(D_LRU)]
        + [_const_spec(c.shape) for c in consts],
        out_specs=rows(D),
        out_shape=jax.ShapeDtypeStruct(x.shape, F32),
        scratch_shapes=[pltpu.VMEM((LRU_CARRY_ROWS, D_LRU), F32), pltpu.VMEM((LRU_CARRY_ROWS, D_LRU), F32),
                        pltpu.VMEM((ts, D_LRU), BF16)],
        compiler_params=pltpu.CompilerParams(
            dimension_semantics=("arbitrary", "arbitrary"), vmem_limit_bytes=V7X_VMEM_LIMIT_BYTES),
        name="layertail",
    )(x, y_att, lx, lg, lx, lg, *consts)


def _pair_heads(fox, moba):
    d = fox.shape[0]
    both = jnp.stack([fox.reshape(d, N_FOX, HEAD_DIM), moba.reshape(d, N_MOBA, HEAD_DIM)], axis=2)
    return both.reshape(d, N_FOX * AUG)


def _transpose_kernel(w_ref, out_ref):
    out_ref[...] = w_ref[...].T.astype(out_ref.dtype)


def _transpose_to_bf16(w, *, tn=256):
    d, n = w.shape
    return pl.pallas_call(
        _transpose_kernel,
        grid=(n // tn,),
        in_specs=[pl.BlockSpec((d, tn), lambda j: (0, j))],
        out_specs=pl.BlockSpec((tn, d), lambda j: (j, 0)),
        out_shape=jax.ShapeDtypeStruct((n, d), BF16),
        compiler_params=pltpu.CompilerParams(
            dimension_semantics=("arbitrary",), vmem_limit_bytes=V7X_VMEM_LIMIT_BYTES),
        name="wtranspose",
    )(w)


def _layer_weights(w_in_l):
    sizes = [N_FOX * HEAD_DIM] * 3 + [N_FOX] + [N_MOBA * HEAD_DIM] * 3 + [D_LRU] * 2
    splits = [sum(sizes[:i + 1]) for i in range(len(sizes) - 1)]
    fq, fk, fv, ff, mq, mk, mv, lx, lg = jnp.split(w_in_l, splits, axis=1)
    wt = _transpose_to_bf16(jnp.concatenate([fq, mq, fv, mv], axis=1))
    ffp = jnp.pad(ff, ((0, 0), (0, AUG - N_FOX)))
    wn = jnp.concatenate([_pair_heads(fk, mk), ffp, lx, lg], axis=1).astype(BF16)
    return wt, wn


def _block_diag(w):
    g, n, _ = w.shape
    eye = jnp.eye(g, dtype=w.dtype)
    return (eye[:, None, :, None] * w[:, :, None, :]).reshape(g * n, g * n)


def _fox_placement():
    place = jnp.zeros((N_SPLIT, AUG, N_FOX, AUG), F32)
    for c in range(N_SPLIT):
        for h in range(N_FOX):
            place = place.at[c, h, h, HEAD_DIM + c].set(1.0)
    return place.reshape(N_SPLIT * AUG, N_FOX * AUG).astype(BF16)


def _tiles(S):
    def fit(n, want):
        t = min(n, want)
        while n % t:
            t //= 2
        return t

    return dict(attn=fit(S, 512), tail=fit(S, 512))


def kernel(x, w_in, b_fgate, conv_w, conv_b, w_rgate, b_rgate, w_igate, b_igate, lru_lambda, out_norm_g, w_out,
           ln1_g, ln1_b, w_ffn_gate, w_ffn_up, w_ffn_down, ln2_g, ln2_b):
    B, S, D = x.shape
    depth = w_in.shape[0]
    assert S % MOBA_BLOCK == 0 and S // MOBA_BLOCK <= AUG - HEAD_DIM
    alpha = (2 * depth) ** 0.25
    t = _tiles(S)
    place = _fox_placement()
    ones_bd = _block_diag(jnp.ones((D_LRU // LRU_GROUP, LRU_GROUP, LRU_GROUP), BF16))
    row = lambda v: v.reshape(1, -1).astype(F32)

    for l in range(depth):
        wt, wn = _layer_weights(w_in[l])
        bf_pad = jnp.pad(b_fgate[l], (0, AUG - N_FOX)).reshape(1, AUG)
        qT_aug, k_aug, vT, lx, lg = _inproj(x, wt, wn, bf_pad, place)

        gain = out_norm_g[l]
        nh = ATTN_HEADS_PER_STEP
        y_att = _attention(qT_aug, k_aug, vT, gain[:D_ATT].reshape(N_ATT // nh, 1, nh * HEAD_DIM),
                           tq=t["attn"], nh=nh)

        lru_params = [conv_w[l], row(conv_b[l]), _block_diag(w_rgate[l]).astype(BF16), row(b_rgate[l]),
                      _block_diag(w_igate[l]).astype(BF16), row(b_igate[l]), row(lru_lambda[l]),
                      row(gain[D_ATT:]), ones_bd]
        wo = w_out[l].astype(BF16)
        x = _layer_tail(x, y_att, lx, lg, lru_params, wo[:D_ATT], wo[D_ATT:], row(ln1_g[l]), row(ln1_b[l]),
                        w_ffn_gate[l].astype(BF16), w_ffn_up[l].astype(BF16), w_ffn_down[l].astype(BF16),
                        row(ln2_g[l]), row(ln2_b[l]), ts=t["tail"], alpha=alpha)
    return x
```

```python
import functools
import math

import jax
import jax.numpy as jnp
from jax import lax
from jax.experimental import pallas as pl
from jax.experimental.pallas import tpu as pltpu

F32 = jnp.float32
BF16 = jnp.bfloat16

HEAD_DIM = 64
AUG = 128
N_FOX = 6
N_MOBA = 6
N_ATT = N_FOX + N_MOBA
D_ATT = N_ATT * HEAD_DIM
D_LRU = 256
LRU_GROUP = 64
MOBA_BLOCK = 256
MOBA_TOPK = 3
CONV_WIDTH = 4
LRU_C = 8.0
LRU_ROWS = 256
LRU_CARRY_ROWS = 8
LN_EPS = 1e-5
RMS_EPS = 1e-6
NEG = -1e30
LOG2E = math.log2(math.e)
QK_SCALE = LOG2E / math.sqrt(HEAD_DIM)
DENOM_ROWS = 16
N_SPLIT = 3
assert N_FOX == N_MOBA
ATTN_HEADS_PER_STEP = 4
ATTN_CHUNKS_PER_ITER = 8
ATTN_ISSUE_ORDER = "SUSUSUSU"

V7X_VMEM_LIMIT_BYTES = 54 * 1024 * 1024

NT_DIMS = (((1,), (1,)), ((), ()))


def _split3(x):
    hi = x.astype(BF16)
    r1 = x - hi.astype(F32)
    mid = r1.astype(BF16)
    lo = (r1 - mid.astype(F32)).astype(BF16)
    return hi, mid, lo


def _layer_norm(z, g, b):
    mu = jnp.mean(z, axis=-1, keepdims=True)
    zc = z - mu
    var = jnp.mean(zc * zc, axis=-1, keepdims=True)
    return zc * lax.rsqrt(var + LN_EPS) * g + b


def _const_spec(shape):
    nd = len(shape)
    return pl.BlockSpec(shape, lambda *_: (0,) * nd, pipeline_mode=pl.Buffered(1))


def _moba_block_bias(gate, cur):
    nblk, tq = gate.shape
    blk = lax.broadcasted_iota(jnp.int32, (nblk, tq), 0)
    past = blk < cur
    g = jnp.where(past, gate, NEG)
    chosen = jnp.zeros((nblk, tq), jnp.bool_)
    for _ in range(MOBA_TOPK):
        top = jnp.max(g, axis=0, keepdims=True)
        first = jnp.min(jnp.where(g == top, blk, nblk), axis=0, keepdims=True)
        pick = blk == first
        chosen = chosen | pick
        g = jnp.where(pick, -jnp.inf, g)
    allowed = (chosen & past) | (blk == cur)
    return jnp.where(allowed, 0.0, NEG)


def _inproj_kernel(x_ref, wt_ref, wn_ref, bf_ref, place_ref, qT_ref, k_ref, vT_ref, lx_ref, lg_ref,
                   tri_ref, fsum_ref, kmean_ref):
    ts = MOBA_BLOCK
    si = pl.program_id(1)

    @pl.when(si == 0)
    def _():
        r = lax.broadcasted_iota(jnp.int32, (ts, ts), 0)
        c = lax.broadcasted_iota(jnp.int32, (ts, ts), 1)
        tri_ref[...] = (c <= r).astype(BF16)
        fsum_ref[...] = jnp.zeros_like(fsum_ref)
        kmean_ref[...] = jnp.zeros_like(kmean_ref)

    xb = x_ref[0].astype(BF16)
    t = lax.dot_general(wt_ref[...], xb, NT_DIMS, preferred_element_type=F32)
    n = jnp.dot(xb, wn_ref[...], preferred_element_type=F32)
    base = N_FOX * AUG
    vT_ref[0] = t[D_ATT:, :].astype(BF16)
    lx_ref[0] = n[:, base + AUG:base + AUG + D_LRU]
    lg_ref[0] = n[:, base + AUG + D_LRU:base + AUG + 2 * D_LRU]

    log_f = jax.nn.log_sigmoid(n[:, base:base + AUG] + bf_ref[...])
    cum = fsum_ref[0:1, :]
    for piece in _split3(log_f):
        cum = cum + jnp.dot(tri_ref[...], piece, preferred_element_type=F32)
    fsum_ref[...] = jnp.broadcast_to(cum[ts - 1:ts, :], fsum_ref.shape)
    pieces = jnp.concatenate(_split3(-LOG2E * cum), axis=1)
    fox_bias = jnp.dot(pieces, place_ref[...], preferred_element_type=F32)

    lane = lax.broadcasted_iota(jnp.int32, (ts, AUG), 1)
    low = lane < HEAD_DIM
    onehot = jnp.where(lane == si, 1.0, 0.0)
    for g in range(N_FOX):
        grp = slice(g * AUG, (g + 1) * AUG)
        kk = n[:, grp]
        k_ref[0, :, grp] = jnp.where(low, kk, fox_bias[:, grp]).astype(BF16)
        k_ref[0, :, (N_FOX + g) * AUG:(N_FOX + g + 1) * AUG] = jnp.where(low, onehot, kk).astype(BF16)

    kmean = kmean_ref[...]
    row = lax.broadcasted_iota(jnp.int32, (HEAD_DIM, ts), 0)
    fox_bias_rows = jnp.where(row < N_SPLIT, 1.0, 0.0).astype(BF16)
    for j in range(N_ATT):
        q = (t[j * HEAD_DIM:(j + 1) * HEAD_DIM, :] * QK_SCALE).astype(BF16)
        if j < N_FOX:
            halves = (q, fox_bias_rows)
        else:
            km = kmean[:, (j - N_FOX) * AUG + HEAD_DIM:(j - N_FOX + 1) * AUG].astype(BF16)
            gate = jnp.dot(km, q, preferred_element_type=F32)
            halves = (_moba_block_bias(gate, si).astype(BF16), q)
        qT_ref[0, j * AUG:j * AUG + HEAD_DIM, :] = halves[0]
        qT_ref[0, j * AUG + HEAD_DIM:(j + 1) * AUG, :] = halves[1]

    block_mean = jnp.mean(n[:, 0:base], axis=0, keepdims=True)
    blk_row = lax.broadcasted_iota(jnp.int32, kmean.shape, 0)
    kmean_ref[...] = jnp.where(blk_row == si, block_mean, kmean)


def _inproj(x, wt, wn, bf_pad, place):
    B, S, D = x.shape
    ts = MOBA_BLOCK
    out_shape = (
        jax.ShapeDtypeStruct((B, N_ATT * AUG, S), BF16),
        jax.ShapeDtypeStruct((B, S, N_ATT * AUG), BF16),
        jax.ShapeDtypeStruct((B, D_ATT, S), BF16),
        jax.ShapeDtypeStruct((B, S, D_LRU), F32),
        jax.ShapeDtypeStruct((B, S, D_LRU), F32),
    )
    return pl.pallas_call(
        _inproj_kernel,
        grid=(B, S // ts),
        in_specs=[
            pl.BlockSpec((1, ts, D), lambda b, s: (b, s, 0)),
            _const_spec(wt.shape), _const_spec(wn.shape), _const_spec(bf_pad.shape), _const_spec(place.shape),
        ],
        out_specs=(
            pl.BlockSpec((1, N_ATT * AUG, ts), lambda b, s: (b, 0, s)),
            pl.BlockSpec((1, ts, N_ATT * AUG), lambda b, s: (b, s, 0)),
            pl.BlockSpec((1, D_ATT, ts), lambda b, s: (b, 0, s)),
            pl.BlockSpec((1, ts, D_LRU), lambda b, s: (b, s, 0)),
            pl.BlockSpec((1, ts, D_LRU), lambda b, s: (b, s, 0)),
        ),
        out_shape=out_shape,
        scratch_shapes=[
            pltpu.VMEM((ts, ts), BF16),
            pltpu.VMEM((LRU_CARRY_ROWS, AUG), F32),
            pltpu.VMEM((HEAD_DIM, N_FOX * AUG), F32),
        ],
        compiler_params=pltpu.CompilerParams(
            dimension_semantics=("arbitrary", "arbitrary"), vmem_limit_bytes=V7X_VMEM_LIMIT_BYTES),
        name="inproj",
    )(x, wt, wn, bf_pad, place)


def _attn_kernel(qT_ref, k_ref, vT_ref, g_ref, out_ref, sa_ref, sb_ref, mxa_ref, mxb_ref, m_ref, acc_ref,
                 *, tq, kw, nh):
    qi = pl.program_id(2)
    heads = range(nh)
    ones = jnp.ones((DENOM_ROWS, kw), BF16)
    bufs = ((sa_ref, mxa_ref), (sb_ref, mxb_ref))

    def score_head(c, hh, s_ref, mx_ref):
        off = pl.multiple_of(c * kw, kw)
        s = jnp.dot(k_ref[0, pl.ds(off, kw), hh * AUG:(hh + 1) * AUG], qT_ref[0, hh * AUG:(hh + 1) * AUG, :],
                    preferred_element_type=F32)
        s_ref[hh] = s
        mx_ref[hh] = jnp.max(s, axis=0, keepdims=True)

    def update_head(c, hh, s_ref, mx_ref, masked):
        off = pl.multiple_of(c * kw, kw)
        if masked:
            kpos = off + lax.broadcasted_iota(jnp.int32, (kw, tq), 0)
            qpos = qi * tq + lax.broadcasted_iota(jnp.int32, (kw, tq), 1)
            s = jnp.where(kpos <= qpos, s_ref[hh], NEG)
            mx = jnp.max(s, axis=0, keepdims=True)
        else:
            s = s_ref[hh]
            mx = mx_ref[hh]
        m = m_ref[hh]
        m_new = jnp.maximum(m, mx)
        alpha = jnp.exp2(m - m_new)
        p = jnp.exp2(s - m_new).astype(BF16)
        vT = vT_ref[0, hh * HEAD_DIM:(hh + 1) * HEAD_DIM, pl.ds(off, kw)]
        v_ones = jnp.concatenate([vT, ones], axis=0)
        acc_ref[hh] = alpha * acc_ref[hh] + jnp.dot(v_ones, p, preferred_element_type=F32)
        m_ref[hh] = m_new

    def stage(c, parity, *, prefetch=True, masked=False):
        todo = {"S": iter(heads), "U": iter(heads)}
        for step in ATTN_ISSUE_ORDER:
            hh = next(todo[step])
            if step == "U":
                update_head(c, hh, *bufs[parity], masked)
            elif prefetch:
                score_head(c + 1, hh, *bufs[1 - parity])

    def past_chunks(first, count):
        for t in range(count):
            stage(first + t, t % 2)

    m_ref[...] = jnp.full(m_ref.shape, -jnp.inf, F32)
    acc_ref[...] = jnp.zeros(acc_ref.shape, F32)
    for hh in heads:
        score_head(0, hh, *bufs[0])

    pairs_per_iter = ATTN_CHUNKS_PER_ITER // 2

    @pl.loop(0, qi // pairs_per_iter)
    def _(j):
        past_chunks(j * ATTN_CHUNKS_PER_ITER, ATTN_CHUNKS_PER_ITER)

    for r in range(1, pairs_per_iter):
        @pl.when(qi % pairs_per_iter == r)
        def _():
            past_chunks((qi - r) * 2, 2 * r)

    stage(2 * qi, 0, masked=True)
    stage(2 * qi + 1, 1, prefetch=False, masked=True)

    normed = []
    for hh in heads:
        acc = acc_ref[hh]
        o = acc[:HEAD_DIM] * (1.0 / acc[HEAD_DIM:HEAD_DIM + 1])
        normed.append(o * lax.rsqrt(jnp.mean(o * o, axis=0, keepdims=True) + RMS_EPS))
    oT = jnp.concatenate(normed, axis=0)
    out_ref[0] = (oT.T * g_ref[0]).astype(out_ref.dtype)


def _attention(qT_aug, k_aug, vT, gain, *, tq, nh):
    B, S, _ = k_aug.shape
    kw = tq // 2
    score_buf = pltpu.VMEM((nh, kw, tq), F32)
    max_buf = pltpu.VMEM((nh, 1, tq), F32)
    acc_buf = pltpu.VMEM((nh, HEAD_DIM + DENOM_ROWS, tq), F32)
    return pl.pallas_call(
        functools.partial(_attn_kernel, tq=tq, kw=kw, nh=nh),
        grid=(B, N_ATT // nh, S // tq),
        in_specs=[
            pl.BlockSpec((1, nh * AUG, tq), lambda b, p, i: (b, p, i)),
            pl.BlockSpec((1, S, nh * AUG), lambda b, p, i: (b, 0, p)),
            pl.BlockSpec((1, nh * HEAD_DIM, S), lambda b, p, i: (b, p, 0), pipeline_mode=pl.Buffered(1)),
            pl.BlockSpec((1, 1, nh * HEAD_DIM), lambda b, p, i: (p, 0, 0)),
        ],
        out_specs=pl.BlockSpec((1, tq, nh * HEAD_DIM), lambda b, p, i: (b, i, p)),
        out_shape=jax.ShapeDtypeStruct((B, S, D_ATT), BF16),
        scratch_shapes=[score_buf, score_buf, max_buf, max_buf, max_buf, acc_buf],
        compiler_params=pltpu.CompilerParams(
            dimension_semantics=("arbitrary", "arbitrary", "arbitrary"),
            vmem_limit_bytes=V7X_VMEM_LIMIT_BYTES),
        name="attention",
    )(qT_aug, k_aug, vT, gain)


def _lru_gates(x, cw_ref, cb_ref, wr_ref, wi_ref, tail_ref):
    tl = x.shape[0]
    xx = jnp.concatenate([tail_ref[...], x], axis=0)
    tail_ref[...] = x[tl - LRU_CARRY_ROWS:tl, :]
    xc = cb_ref[...] + x * cw_ref[CONV_WIDTH - 1:CONV_WIDTH, :]
    for d in range(1, CONV_WIDTH):
        lo = LRU_CARRY_ROWS - d
        xc = xc + xx[lo:lo + tl, :] * cw_ref[CONV_WIDTH - 1 - d:CONV_WIDTH - d, :]
    xcb = xc.astype(BF16)
    return (xc, jnp.dot(xcb, wr_ref[...], preferred_element_type=F32),
            jnp.dot(xcb, wi_ref[...], preferred_element_type=F32))


def _lru_scan(gates, lg, br_ref, bi_ref, lam_ref, h_ref):
    xc, r_pre, i_pre = gates
    tl = xc.shape[0]
    r = jax.nn.sigmoid(r_pre + br_ref[...])
    i = jax.nn.sigmoid(i_pre + bi_ref[...])
    log_a = LRU_C * r * jax.nn.log_sigmoid(lam_ref[...])
    a = jnp.exp(log_a)
    u = jnp.sqrt(-jnp.tanh(log_a) * (a * a + 1.0)) * (i * xc)

    pos = lax.broadcasted_iota(jnp.int32, (tl, D_LRU), 0)
    d = 1
    while d < tl:
        keep = pos >= d
        a_prev = jnp.where(keep, pltpu.roll(a, d, 0), 1.0)
        u_prev = jnp.where(keep, pltpu.roll(u, d, 0), 0.0)
        u = a * u_prev + u
        a = a * a_prev
        d *= 2
    h = u + a * h_ref[0:1, :]
    h_ref[...] = jnp.broadcast_to(h[tl - 1:tl, :], h_ref.shape)
    return jax.nn.gelu(lg) * h


def _lru_norm(y, g_ref, ones_ref):
    sq_hi, sq_mid, _ = _split3(y * y)
    ms = (jnp.dot(sq_hi, ones_ref[...], preferred_element_type=F32)
          + jnp.dot(sq_mid, ones_ref[...], preferred_element_type=F32)) * (1.0 / LRU_GROUP)
    return y * lax.rsqrt(ms + RMS_EPS) * g_ref[...]


def _tail_kernel(x_ref, ya_ref, lx0_ref, lg0_ref, lx_ref, lg_ref, cw_ref, cb_ref, wr_ref, br_ref, wi_ref, bi_ref,
                 lam_ref, gl_ref, ones_ref, wa_ref, wl_ref, g1_ref, b1_ref, wg_ref, wu_ref, wd_ref, g2_ref, b2_ref,
                 out_ref, conv_tail_ref, h_ref, ylru_ref, *, alpha, ts):
    halves = list(range(0, ts, LRU_ROWS))

    def lru_gates(lx_t, r0):
        return _lru_gates(lx_t[0, r0:r0 + LRU_ROWS, :], cw_ref, cb_ref, wr_ref, wi_ref, conv_tail_ref)

    def lru_scan(gates, lg_t, r0):
        return _lru_scan(gates, lg_t[0, r0:r0 + LRU_ROWS, :], br_ref, bi_ref, lam_ref, h_ref)

    def lru_store(ys):
        ylru_ref[...] = jnp.concatenate([_lru_norm(y, gl_ref, ones_ref) for y in ys], axis=0).astype(BF16)

    @pl.when(pl.program_id(1) == 0)
    def _():
        conv_tail_ref[...] = jnp.zeros_like(conv_tail_ref)
        h_ref[...] = jnp.zeros_like(h_ref)
        lru_store([lru_scan(lru_gates(lx0_ref, r0), lg0_ref, r0) for r0 in halves])

    y_lru = ylru_ref[...]
    next_gates = [lru_gates(lx_ref, r0) for r0 in halves]

    mix = jnp.dot(ya_ref[0], wa_ref[...], preferred_element_type=F32)
    mix = mix + jnp.dot(y_lru, wl_ref[...], preferred_element_type=F32)
    x1 = _layer_norm(alpha * x_ref[0] + mix, g1_ref[...], b1_ref[...])
    xb = x1.astype(BF16)

    next_y = [lru_scan(next_gates[0], lg_ref, halves[0])]
    gate = jnp.dot(xb, wg_ref[...], preferred_element_type=F32)
    up = jnp.dot(xb, wu_ref[...], preferred_element_type=F32)
    next_y += [lru_scan(g, lg_ref, r0) for g, r0 in zip(next_gates[1:], halves[1:])]
    hid = (jax.nn.silu(gate) * up).astype(BF16)
    ffn = jnp.dot(hid, wd_ref[...], preferred_element_type=F32)
    out_ref[0] = _layer_norm(alpha * x1 + ffn, g2_ref[...], b2_ref[...])

    lru_store(next_y)


def _layer_tail(x, y_att, lx, lg, lru_params, wa, wl, g1, b1, wg, wu, wd, g2, b2, *, ts, alpha):
    B, S, D = x.shape
    last = S // ts - 1
    consts = [*lru_params, wa, wl, g1, b1, wg, wu, wd, g2, b2]
    rows = lambda width: pl.BlockSpec((1, ts, width), lambda b_, s: (b_, s, 0))
    next_rows = lambda width: pl.BlockSpec((1, ts, width), lambda b_, s: (b_, jnp.minimum(s + 1, last), 0))
    return pl.pallas_call(
        functools.partial(_tail_kernel, alpha=alpha, ts=ts),
        grid=(B, S // ts),
        in_specs=[rows(D), rows(D_ATT), rows(D_LRU), rows(D_LRU), next_rows(D_LRU), next_rows(D_LRU)]
        + [_const_spec(c.shape) for c in consts],
        out_specs=rows(D),
        out_shape=jax.ShapeDtypeStruct(x.shape, F32),
        scratch_shapes=[pltpu.VMEM((LRU_CARRY_ROWS, D_LRU), F32), pltpu.VMEM((LRU_CARRY_ROWS, D_LRU), F32),
                        pltpu.VMEM((ts, D_LRU), BF16)],
        compiler_params=pltpu.CompilerParams(
            dimension_semantics=("arbitrary", "arbitrary"), vmem_limit_bytes=V7X_VMEM_LIMIT_BYTES),
        name="layertail",
    )(x, y_att, lx, lg, lx, lg, *consts)


def _pair_heads(fox, moba):
    d = fox.shape[0]
    both = jnp.stack([fox.reshape(d, N_FOX, HEAD_DIM), moba.reshape(d, N_MOBA, HEAD_DIM)], axis=2)
    return both.reshape(d, N_FOX * AUG)


def _transpose_kernel(w_ref, out_ref):
    out_ref[...] = w_ref[...].T.astype(out_ref.dtype)


def _transpose_to_bf16(w, *, tn=256):
    d, n = w.shape
    return pl.pallas_call(
        _transpose_kernel,
        grid=(n // tn,),
        in_specs=[pl.BlockSpec((d, tn), lambda j: (0, j))],
        out_specs=pl.BlockSpec((tn, d), lambda j: (j, 0)),
        out_shape=jax.ShapeDtypeStruct((n, d), BF16),
        compiler_params=pltpu.CompilerParams(
            dimension_semantics=("arbitrary",), vmem_limit_bytes=V7X_VMEM_LIMIT_BYTES),
        name="wtranspose",
    )(w)


def _layer_weights(w_in_l):
    sizes = [N_FOX * HEAD_DIM] * 3 + [N_FOX] + [N_MOBA * HEAD_DIM] * 3 + [D_LRU] * 2
    splits = [sum(sizes[:i + 1]) for i in range(len(sizes) - 1)]
    fq, fk, fv, ff, mq, mk, mv, lx, lg = jnp.split(w_in_l, splits, axis=1)
    wt = _transpose_to_bf16(jnp.concatenate([fq, mq, fv, mv], axis=1))
    ffp = jnp.pad(ff, ((0, 0), (0, AUG - N_FOX)))
    wn = jnp.concatenate([_pair_heads(fk, mk), ffp, lx, lg], axis=1).astype(BF16)
    return wt, wn


def _block_diag(w):
    g, n, _ = w.shape
    eye = jnp.eye(g, dtype=w.dtype)
    return (eye[:, None, :, None] * w[:, :, None, :]).reshape(g * n, g * n)


def _fox_placement():
    place = jnp.zeros((N_SPLIT, AUG, N_FOX, AUG), F32)
    for c in range(N_SPLIT):
        for h in range(N_FOX):
            place = place.at[c, h, h, HEAD_DIM + c].set(1.0)
    return place.reshape(N_SPLIT * AUG, N_FOX * AUG).astype(BF16)


def _tiles(S):
    def fit(n, want):
        t = min(n, want)
        while n % t:
            t //= 2
        return t

    return dict(attn=fit(S, 512), tail=fit(S, 512))


def kernel(x, w_in, b_fgate, conv_w, conv_b, w_rgate, b_rgate, w_igate, b_igate, lru_lambda, out_norm_g, w_out,
           ln1_g, ln1_b, w_ffn_gate, w_ffn_up, w_ffn_down, ln2_g, ln2_b):
    B, S, D = x.shape
    depth = w_in.shape[0]
    assert S % MOBA_BLOCK == 0 and S // MOBA_BLOCK <= AUG - HEAD_DIM
    alpha = (2 * depth) ** 0.25
    t = _tiles(S)
    place = _fox_placement()
    ones_bd = _block_diag(jnp.ones((D_LRU // LRU_GROUP, LRU_GROUP, LRU_GROUP), BF16))
    row = lambda v: v.reshape(1, -1).astype(F32)

    for l in range(depth):
        wt, wn = _layer_weights(w_in[l])
        bf_pad = jnp.pad(b_fgate[l], (0, AUG - N_FOX)).reshape(1, AUG)
        qT_aug, k_aug, vT, lx, lg = _inproj(x, wt, wn, bf_pad, place)

        gain = out_norm_g[l]
        nh = ATTN_HEADS_PER_STEP
        y_att = _attention(qT_aug, k_aug, vT, gain[:D_ATT].reshape(N_ATT // nh, 1, nh * HEAD_DIM),
                           tq=t["attn"], nh=nh)

        lru_params = [conv_w[l], row(conv_b[l]), _block_diag(w_rgate[l]).astype(BF16), row(b_rgate[l]),
                      _block_diag(w_igate[l]).astype(BF16), row(b_igate[l]), row(lru_lambda[l]),
                      row(gain[D_ATT:]), ones_bd]
        wo = w_out[l].astype(BF16)
        x = _layer_tail(x, y_att, lx, lg, lru_params, wo[:D_ATT], wo[D_ATT:], row(ln1_g[l]), row(ln1_b[l]),
                        w_ffn_gate[l].astype(BF16), w_ffn_up[l].astype(BF16), w_ffn_down[l].astype(BF16),
                        row(ln2_g[l]), row(ln2_b[l]), ts=t["tail"], alpha=alpha)
    return x
```

```python
import functools
import math

import jax
import jax.numpy as jnp
from jax import lax
from jax.experimental import pallas as pl
from jax.experimental.pallas import tpu as pltpu

F32 = jnp.float32
BF16 = jnp.bfloat16

HEAD_DIM = 64
AUG = 128
N_FOX = 6
N_MOBA = 6
N_ATT = N_FOX + N_MOBA
D_ATT = N_ATT * HEAD_DIM
D_LRU = 256
LRU_GROUP = 64
MOBA_BLOCK = 256
MOBA_TOPK = 3
CONV_WIDTH = 4
LRU_C = 8.0
LRU_ROWS = 256
LRU_CARRY_ROWS = 8
LN_EPS = 1e-5
RMS_EPS = 1e-6
NEG = -1e30
LOG2E = math.log2(math.e)
QK_SCALE = LOG2E / math.sqrt(HEAD_DIM)
DENOM_ROWS = 16
N_SPLIT = 3
assert N_FOX == N_MOBA
ATTN_HEADS_PER_STEP = 4
ATTN_CHUNKS_PER_ITER = 8
ATTN_ISSUE_ORDER = "SUSUSUSU"

V7X_VMEM_LIMIT_BYTES = 48 * 1024 * 1024

NT_DIMS = (((1,), (1,)), ((), ()))


def _split3(x):
    hi = x.astype(BF16)
    r1 = x - hi.astype(F32)
    mid = r1.astype(BF16)
    lo = (r1 - mid.astype(F32)).astype(BF16)
    return hi, mid, lo


def _layer_norm(z, g, b):
    mu = jnp.mean(z, axis=-1, keepdims=True)
    zc = z - mu
    var = jnp.mean(zc * zc, axis=-1, keepdims=True)
    return zc * lax.rsqrt(var + LN_EPS) * g + b


def _const_spec(shape):
    nd = len(shape)
    return pl.BlockSpec(shape, lambda *_: (0,) * nd, pipeline_mode=pl.Buffered(1))


def _moba_block_bias(gate, cur):
    nblk, tq = gate.shape
    blk = lax.broadcasted_iota(jnp.int32, (nblk, tq), 0)
    past = blk < cur
    g = jnp.where(past, gate, NEG)
    chosen = jnp.zeros((nblk, tq), jnp.bool_)
    for _ in range(MOBA_TOPK):
        top = jnp.max(g, axis=0, keepdims=True)
        first = jnp.min(jnp.where(g == top, blk, nblk), axis=0, keepdims=True)
        pick = blk == first
        chosen = chosen | pick
        g = jnp.where(pick, -jnp.inf, g)
    allowed = (chosen & past) | (blk == cur)
    return jnp.where(allowed, 0.0, NEG)


def _inproj_kernel(x_ref, wt_ref, wn_ref, bf_ref, place_ref, qT_ref, k_ref, vT_ref, lx_ref, lg_ref,
                   tri_ref, fsum_ref, kmean_ref):
    ts = MOBA_BLOCK
    si = pl.program_id(1)

    @pl.when(si == 0)
    def _():
        r = lax.broadcasted_iota(jnp.int32, (ts, ts), 0)
        c = lax.broadcasted_iota(jnp.int32, (ts, ts), 1)
        tri_ref[...] = (c <= r).astype(BF16)
        fsum_ref[...] = jnp.zeros_like(fsum_ref)
        kmean_ref[...] = jnp.zeros_like(kmean_ref)

    xb = x_ref[0].astype(BF16)
    t = lax.dot_general(wt_ref[...], xb, NT_DIMS, preferred_element_type=F32)
    n = jnp.dot(xb, wn_ref[...], preferred_element_type=F32)
    base = N_FOX * AUG
    vT_ref[0] = t[D_ATT:, :].astype(BF16)
    lx_ref[0] = n[:, base + AUG:base + AUG + D_LRU]
    lg_ref[0] = n[:, base + AUG + D_LRU:base + AUG + 2 * D_LRU]

    log_f = jax.nn.log_sigmoid(n[:, base:base + AUG] + bf_ref[...])
    cum = fsum_ref[0:1, :]
    for piece in _split3(log_f):
        cum = cum + jnp.dot(tri_ref[...], piece, preferred_element_type=F32)
    fsum_ref[...] = jnp.broadcast_to(cum[ts - 1:ts, :], fsum_ref.shape)
    pieces = jnp.concatenate(_split3(-LOG2E * cum), axis=1)
    fox_bias = jnp.dot(pieces, place_ref[...], preferred_element_type=F32)

    lane = lax.broadcasted_iota(jnp.int32, (ts, AUG), 1)
    low = lane < HEAD_DIM
    onehot = jnp.where(lane == si, 1.0, 0.0)
    for g in range(N_FOX):
        grp = slice(g * AUG, (g + 1) * AUG)
        kk = n[:, grp]
        k_ref[0, :, grp] = jnp.where(low, kk, fox_bias[:, grp]).astype(BF16)
        k_ref[0, :, (N_FOX + g) * AUG:(N_FOX + g + 1) * AUG] = jnp.where(low, onehot, kk).astype(BF16)

    kmean = kmean_ref[...]
    row = lax.broadcasted_iota(jnp.int32, (HEAD_DIM, ts), 0)
    fox_bias_rows = jnp.where(row < N_SPLIT, 1.0, 0.0).astype(BF16)
    for j in range(N_ATT):
        q = (t[j * HEAD_DIM:(j + 1) * HEAD_DIM, :] * QK_SCALE).astype(BF16)
        if j < N_FOX:
            halves = (q, fox_bias_rows)
        else:
            km = kmean[:, (j - N_FOX) * AUG + HEAD_DIM:(j - N_FOX + 1) * AUG].astype(BF16)
            gate = jnp.dot(km, q, preferred_element_type=F32)
            halves = (_moba_block_bias(gate, si).astype(BF16), q)
        qT_ref[0, j * AUG:j * AUG + HEAD_DIM, :] = halves[0]
        qT_ref[0, j * AUG + HEAD_DIM:(j + 1) * AUG, :] = halves[1]

    block_mean = jnp.mean(n[:, 0:base], axis=0, keepdims=True)
    blk_row = lax.broadcasted_iota(jnp.int32, kmean.shape, 0)
    kmean_ref[...] = jnp.where(blk_row == si, block_mean, kmean)


def _inproj(x, wt, wn, bf_pad, place):
    B, S, D = x.shape
    ts = MOBA_BLOCK
    out_shape = (
        jax.ShapeDtypeStruct((B, N_ATT * AUG, S), BF16),
        jax.ShapeDtypeStruct((B, S, N_ATT * AUG), BF16),
        jax.ShapeDtypeStruct((B, D_ATT, S), BF16),
        jax.ShapeDtypeStruct((B, S, D_LRU), F32),
        jax.ShapeDtypeStruct((B, S, D_LRU), F32),
    )
    return pl.pallas_call(
        _inproj_kernel,
        grid=(B, S // ts),
        in_specs=[
            pl.BlockSpec((1, ts, D), lambda b, s: (b, s, 0)),
            _const_spec(wt.shape), _const_spec(wn.shape), _const_spec(bf_pad.shape), _const_spec(place.shape),
        ],
        out_specs=(
            pl.BlockSpec((1, N_ATT * AUG, ts), lambda b, s: (b, 0, s)),
            pl.BlockSpec((1, ts, N_ATT * AUG), lambda b, s: (b, s, 0)),
            pl.BlockSpec((1, D_ATT, ts), lambda b, s: (b, 0, s)),
            pl.BlockSpec((1, ts, D_LRU), lambda b, s: (b, s, 0)),
            pl.BlockSpec((1, ts, D_LRU), lambda b, s: (b, s, 0)),
        ),
        out_shape=out_shape,
        scratch_shapes=[
            pltpu.VMEM((ts, ts), BF16),
            pltpu.VMEM((LRU_CARRY_ROWS, AUG), F32),
            pltpu.VMEM((HEAD_DIM, N_FOX * AUG), F32),
        ],
        compiler_params=pltpu.CompilerParams(
            dimension_semantics=("arbitrary", "arbitrary"), vmem_limit_bytes=V7X_VMEM_LIMIT_BYTES),
        name="inproj",
    )(x, wt, wn, bf_pad, place)


def _attn_kernel(qT_ref, k_ref, vT_ref, g_ref, out_ref, sa_ref, sb_ref, mxa_ref, mxb_ref, m_ref, acc_ref,
                 *, tq, kw, nh):
    qi = pl.program_id(2)
    heads = range(nh)
    ones = jnp.ones((DENOM_ROWS, kw), BF16)
    bufs = ((sa_ref, mxa_ref), (sb_ref, mxb_ref))

    def score_head(c, hh, s_ref, mx_ref):
        off = pl.multiple_of(c * kw, kw)
        s = jnp.dot(k_ref[0, pl.ds(off, kw), hh * AUG:(hh + 1) * AUG], qT_ref[0, hh * AUG:(hh + 1) * AUG, :],
                    preferred_element_type=F32)
        s_ref[hh] = s
        mx_ref[hh] = jnp.max(s, axis=0, keepdims=True)

    def update_head(c, hh, s_ref, mx_ref, masked):
        off = pl.multiple_of(c * kw, kw)
        if masked:
            kpos = off + lax.broadcasted_iota(jnp.int32, (kw, tq), 0)
            qpos = qi * tq + lax.broadcasted_iota(jnp.int32, (kw, tq), 1)
            s = jnp.where(kpos <= qpos, s_ref[hh], NEG)
            mx = jnp.max(s, axis=0, keepdims=True)
        else:
            s = s_ref[hh]
            mx = mx_ref[hh]
        m = m_ref[hh]
        m_new = jnp.maximum(m, mx)
        alpha = jnp.exp2(m - m_new)
        p = jnp.exp2(s - m_new).astype(BF16)
        vT = vT_ref[0, hh * HEAD_DIM:(hh + 1) * HEAD_DIM, pl.ds(off, kw)]
        v_ones = jnp.concatenate([vT, ones], axis=0)
        acc_ref[hh] = alpha * acc_ref[hh] + jnp.dot(v_ones, p, preferred_element_type=F32)
        m_ref[hh] = m_new

    def stage(c, parity, *, prefetch=True, masked=False):
        todo = {"S": iter(heads), "U": iter(heads)}
        for step in ATTN_ISSUE_ORDER:
            hh = next(todo[step])
            if step == "U":
                update_head(c, hh, *bufs[parity], masked)
            elif prefetch:
                score_head(c + 1, hh, *bufs[1 - parity])

    def past_chunks(first, count):
        for t in range(count):
            stage(first + t, t % 2)

    m_ref[...] = jnp.full(m_ref.shape, -jnp.inf, F32)
    acc_ref[...] = jnp.zeros(acc_ref.shape, F32)
    for hh in heads:
        score_head(0, hh, *bufs[0])

    pairs_per_iter = ATTN_CHUNKS_PER_ITER // 2

    @pl.loop(0, qi // pairs_per_iter)
    def _(j):
        past_chunks(j * ATTN_CHUNKS_PER_ITER, ATTN_CHUNKS_PER_ITER)

    for r in range(1, pairs_per_iter):
        @pl.when(qi % pairs_per_iter == r)
        def _():
            past_chunks((qi - r) * 2, 2 * r)

    stage(2 * qi, 0, masked=True)
    stage(2 * qi + 1, 1, prefetch=False, masked=True)

    normed = []
    for hh in heads:
        acc = acc_ref[hh]
        o = acc[:HEAD_DIM] * (1.0 / acc[HEAD_DIM:HEAD_DIM + 1])
        normed.append(o * lax.rsqrt(jnp.mean(o * o, axis=0, keepdims=True) + RMS_EPS))
    oT = jnp.concatenate(normed, axis=0)
    out_ref[0] = (oT.T * g_ref[0]).astype(out_ref.dtype)


def _attention(qT_aug, k_aug, vT, gain, *, tq, nh):
    B, S, _ = k_aug.shape
    kw = tq // 2
    score_buf = pltpu.VMEM((nh, kw, tq), F32)
    max_buf = pltpu.VMEM((nh, 1, tq), F32)
    acc_buf = pltpu.VMEM((nh, HEAD_DIM + DENOM_ROWS, tq), F32)
    return pl.pallas_call(
        functools.partial(_attn_kernel, tq=tq, kw=kw, nh=nh),
        grid=(B, N_ATT // nh, S // tq),
        in_specs=[
            pl.BlockSpec((1, nh * AUG, tq), lambda b, p, i: (b, p, i)),
            pl.BlockSpec((1, S, nh * AUG), lambda b, p, i: (b, 0, p), pipeline_mode=pl.Buffered(1)),
            pl.BlockSpec((1, nh * HEAD_DIM, S), lambda b, p, i: (b, p, 0), pipeline_mode=pl.Buffered(1)),
            pl.BlockSpec((1, 1, nh * HEAD_DIM), lambda b, p, i: (p, 0, 0)),
        ],
        out_specs=pl.BlockSpec((1, tq, nh * HEAD_DIM), lambda b, p, i: (b, i, p)),
        out_shape=jax.ShapeDtypeStruct((B, S, D_ATT), BF16),
        scratch_shapes=[score_buf, score_buf, max_buf, max_buf, max_buf, acc_buf],
        compiler_params=pltpu.CompilerParams(
            dimension_semantics=("arbitrary", "arbitrary", "arbitrary"),
            vmem_limit_bytes=V7X_VMEM_LIMIT_BYTES),
        name="attention",
    )(qT_aug, k_aug, vT, gain)


def _lru_gates(x, cw_ref, cb_ref, wr_ref, wi_ref, tail_ref):
    tl = x.shape[0]
    xx = jnp.concatenate([tail_ref[...], x], axis=0)
    tail_ref[...] = x[tl - LRU_CARRY_ROWS:tl, :]
    xc = cb_ref[...] + x * cw_ref[CONV_WIDTH - 1:CONV_WIDTH, :]
    for d in range(1, CONV_WIDTH):
        lo = LRU_CARRY_ROWS - d
        xc = xc + xx[lo:lo + tl, :] * cw_ref[CONV_WIDTH - 1 - d:CONV_WIDTH - d, :]
    xcb = xc.astype(BF16)
    return (xc, jnp.dot(xcb, wr_ref[...], preferred_element_type=F32),
            jnp.dot(xcb, wi_ref[...], preferred_element_type=F32))


def _lru_scan(gates, lg, br_ref, bi_ref, lam_ref, h_ref):
    xc, r_pre, i_pre = gates
    tl = xc.shape[0]
    r = jax.nn.sigmoid(r_pre + br_ref[...])
    i = jax.nn.sigmoid(i_pre + bi_ref[...])
    log_a = LRU_C * r * jax.nn.log_sigmoid(lam_ref[...])
    a = jnp.exp(log_a)
    u = jnp.sqrt(-jnp.tanh(log_a) * (a * a + 1.0)) * (i * xc)

    pos = lax.broadcasted_iota(jnp.int32, (tl, D_LRU), 0)
    d = 1
    while d < tl:
        keep = pos >= d
        a_prev = jnp.where(keep, pltpu.roll(a, d, 0), 1.0)
        u_prev = jnp.where(keep, pltpu.roll(u, d, 0), 0.0)
        u = a * u_prev + u
        a = a * a_prev
        d *= 2
    h = u + a * h_ref[0:1, :]
    h_ref[...] = jnp.broadcast_to(h[tl - 1:tl, :], h_ref.shape)
    return jax.nn.gelu(lg) * h


def _lru_norm(y, g_ref, ones_ref):
    sq_hi, sq_mid, _ = _split3(y * y)
    ms = (jnp.dot(sq_hi, ones_ref[...], preferred_element_type=F32)
          + jnp.dot(sq_mid, ones_ref[...], preferred_element_type=F32)) * (1.0 / LRU_GROUP)
    return y * lax.rsqrt(ms + RMS_EPS) * g_ref[...]


def _tail_kernel(x_ref, ya_ref, lx0_ref, lg0_ref, lx_ref, lg_ref, cw_ref, cb_ref, wr_ref, br_ref, wi_ref, bi_ref,
                 lam_ref, gl_ref, ones_ref, wa_ref, wl_ref, g1_ref, b1_ref, wg_ref, wu_ref, wd_ref, g2_ref, b2_ref,
                 out_ref, conv_tail_ref, h_ref, ylru_ref, *, alpha, ts):
    halves = list(range(0, ts, LRU_ROWS))

    def lru_gates(lx_t, r0):
        return _lru_gates(lx_t[0, r0:r0 + LRU_ROWS, :], cw_ref, cb_ref, wr_ref, wi_ref, conv_tail_ref)

    def lru_scan(gates, lg_t, r0):
        return _lru_scan(gates, lg_t[0, r0:r0 + LRU_ROWS, :], br_ref, bi_ref, lam_ref, h_ref)

    def lru_store(ys):
        ylru_ref[...] = jnp.concatenate([_lru_norm(y, gl_ref, ones_ref) for y in ys], axis=0).astype(BF16)

    @pl.when(pl.program_id(1) == 0)
    def _():
        conv_tail_ref[...] = jnp.zeros_like(conv_tail_ref)
        h_ref[...] = jnp.zeros_like(h_ref)
        lru_store([lru_scan(lru_gates(lx0_ref, r0), lg0_ref, r0) for r0 in halves])

    y_lru = ylru_ref[...]
    next_gates = [lru_gates(lx_ref, r0) for r0 in halves]

    parts = [slice(r0, r0 + ts // 2) for r0 in (0, ts // 2)]
    mix = [jnp.dot(ya_ref[0, p, :], wa_ref[...], preferred_element_type=F32)
           + jnp.dot(y_lru[p, :], wl_ref[...], preferred_element_type=F32) for p in parts]
    x1 = [None, None]
    gate = [None, None]
    up = [None, None]
    next_y = []
    for h, p in enumerate(parts):
        x1[h] = _layer_norm(alpha * x_ref[0, p, :] + mix[h], g1_ref[...], b1_ref[...])
        xb = x1[h].astype(BF16)
        if h < len(next_gates):
            next_y.append(lru_scan(next_gates[h], lg_ref, halves[h]))
        gate[h] = jnp.dot(xb, wg_ref[...], preferred_element_type=F32)
        up[h] = jnp.dot(xb, wu_ref[...], preferred_element_type=F32)
    next_y += [lru_scan(g, lg_ref, r0) for g, r0 in zip(next_gates[len(parts):], halves[len(parts):])]
    for h, p in enumerate(parts):
        hid = (jax.nn.silu(gate[h]) * up[h]).astype(BF16)
        ffn = jnp.dot(hid, wd_ref[...], preferred_element_type=F32)
        out_ref[0, p, :] = _layer_norm(alpha * x1[h] + ffn, g2_ref[...], b2_ref[...])

    lru_store(next_y)


def _layer_tail(x, y_att, lx, lg, lru_params, wa, wl, g1, b1, wg, wu, wd, g2, b2, *, ts, alpha):
    B, S, D = x.shape
    last = S // ts - 1
    consts = [*lru_params, wa, wl, g1, b1, wg, wu, wd, g2, b2]
    rows = lambda width: pl.BlockSpec((1, ts, width), lambda b_, s: (b_, s, 0))
    next_rows = lambda width: pl.BlockSpec((1, ts, width), lambda b_, s: (b_, jnp.minimum(s + 1, last), 0))
    return pl.pallas_call(
        functools.partial(_tail_kernel, alpha=alpha, ts=ts),
        grid=(B, S // ts),
        in_specs=[rows(D), rows(D_ATT), rows(D_LRU), rows(D_LRU), next_rows(D_LRU), next_rows(D_LRU)]
        + [_const_spec(c.shape) for c in consts],
        out_specs=rows(D),
        out_shape=jax.ShapeDtypeStruct(x.shape, F32),
        scratch_shapes=[pltpu.VMEM((LRU_CARRY_ROWS, D_LRU), F32), pltpu.VMEM((LRU_CARRY_ROWS, D_LRU), F32),
                        pltpu.VMEM((ts, D_LRU), BF16)],
        compiler_params=pltpu.CompilerParams(
            dimension_semantics=("arbitrary", "arbitrary"), vmem_limit_bytes=V7X_VMEM_LIMIT_BYTES),
        name="layertail",
    )(x, y_att, lx, lg, lx, lg, *consts)


def _pair_heads(fox, moba):
    d = fox.shape[0]
    both = jnp.stack([fox.reshape(d, N_FOX, HEAD_DIM), moba.reshape(d, N_MOBA, HEAD_DIM)], axis=2)
    return both.reshape(d, N_FOX * AUG)


def _transpose_kernel(w_ref, out_ref):
    out_ref[...] = w_ref[...].T.astype(out_ref.dtype)


def _transpose_to_bf16(w, *, tn=256):
    d, n = w.shape
    return pl.pallas_call(
        _transpose_kernel,
        grid=(n // tn,),
        in_specs=[pl.BlockSpec((d, tn), lambda j: (0, j))],
        out_specs=pl.BlockSpec((tn, d), lambda j: (j, 0)),
        out_shape=jax.ShapeDtypeStruct((n, d), BF16),
        compiler_params=pltpu.CompilerParams(
            dimension_semantics=("arbitrary",), vmem_limit_bytes=V7X_VMEM_LIMIT_BYTES),
        name="wtranspose",
    )(w)


def _layer_weights(w_in_l):
    sizes = [N_FOX * HEAD_DIM] * 3 + [N_FOX] + [N_MOBA * HEAD_DIM] * 3 + [D_LRU] * 2
    splits = [sum(sizes[:i + 1]) for i in range(len(sizes) - 1)]
    fq, fk, fv, ff, mq, mk, mv, lx, lg = jnp.split(w_in_l, splits, axis=1)
    wt = _transpose_to_bf16(jnp.concatenate([fq, mq, fv, mv], axis=1))
    ffp = jnp.pad(ff, ((0, 0), (0, AUG - N_FOX)))
    wn = jnp.concatenate([_pair_heads(fk, mk), ffp, lx, lg], axis=1).astype(BF16)
    return wt, wn


def _block_diag(w):
    g, n, _ = w.shape
    eye = jnp.eye(g, dtype=w.dtype)
    return (eye[:, None, :, None] * w[:, :, None, :]).reshape(g * n, g * n)


def _fox_placement():
    place = jnp.zeros((N_SPLIT, AUG, N_FOX, AUG), F32)
    for c in range(N_SPLIT):
        for h in range(N_FOX):
            place = place.at[c, h, h, HEAD_DIM + c].set(1.0)
    return place.reshape(N_SPLIT * AUG, N_FOX * AUG).astype(BF16)


def _tiles(S):
    def fit(n, want):
        t = min(n, want)
        while n % t:
            t //= 2
        return t

    return dict(attn=fit(S, 512), tail=fit(S, 512))


def kernel(x, w_in, b_fgate, conv_w, conv_b, w_rgate, b_rgate, w_igate, b_igate, lru_lambda, out_norm_g, w_out,
           ln1_g, ln1_b, w_ffn_gate, w_ffn_up, w_ffn_down, ln2_g, ln2_b):
    B, S, D = x.shape
    depth = w_in.shape[0]
    assert S % MOBA_BLOCK == 0 and S // MOBA_BLOCK <= AUG - HEAD_DIM
    alpha = (2 * depth) ** 0.25
    t = _tiles(S)
    place = _fox_placement()
    ones_bd = _block_diag(jnp.ones((D_LRU // LRU_GROUP, LRU_GROUP, LRU_GROUP), BF16))
    row = lambda v: v.reshape(1, -1).astype(F32)

    for l in range(depth):
        wt, wn = _layer_weights(w_in[l])
        bf_pad = jnp.pad(b_fgate[l], (0, AUG - N_FOX)).reshape(1, AUG)
        qT_aug, k_aug, vT, lx, lg = _inproj(x, wt, wn, bf_pad, place)

        gain = out_norm_g[l]
        nh = ATTN_HEADS_PER_STEP
        y_att = _attention(qT_aug, k_aug, vT, gain[:D_ATT].reshape(N_ATT // nh, 1, nh * HEAD_DIM),
                           tq=t["attn"], nh=nh)

        lru_params = [conv_w[l], row(conv_b[l]), _block_diag(w_rgate[l]).astype(BF16), row(b_rgate[l]),
                      _block_diag(w_igate[l]).astype(BF16), row(b_igate[l]), row(lru_lambda[l]),
                      row(gain[D_ATT:]), ones_bd]
        wo = w_out[l].astype(BF16)
        x = _layer_tail(x, y_att, lx, lg, lru_params, wo[:D_ATT], wo[D_ATT:], row(ln1_g[l]), row(ln1_b[l]),
                        w_ffn_gate[l].astype(BF16), w_ffn_up[l].astype(BF16), w_ffn_down[l].astype(BF16),
                        row(ln2_g[l]), row(ln2_b[l]), ts=t["tail"], alpha=alpha)
    return x
```
